```python
import math
import jax, jax.numpy as jnp
from jax import lax
import numpy as np

D_MODEL = 1024
BATCH = 16
SEQ = 2048
DEPTH = 2

D_BRANCH = 512
D_MIX = 3 * D_BRANCH
A_HEADS = 4
A_QK_DIM = 64
A_V_DIM = 2 * A_QK_DIM
A_QK_COLS = A_HEADS * 2 * A_QK_DIM
A_V_COLS = A_HEADS * A_V_DIM
B_HEADS = 4
B_QK_DIM = 64
B_V_DIM = 128
B_QK_COLS = B_HEADS * B_QK_DIM
B_V_COLS = B_HEADS * B_V_DIM
RET_CHUNK = 128
ROPE_BASE = 10000.0
C_GROUPS = 4
C_GROUP_DIM = D_BRANCH // C_GROUPS
POOL_WINDOWS = (2, 4, 8, 16)
NUM_BUCKETS = 32
MAX_DISTANCE = 128
Q_BLOCK = 128
EPS = 1e-6
COLUMN_SIZES = (A_QK_COLS, A_QK_COLS, A_V_COLS, D_BRANCH,
                B_QK_COLS, B_QK_COLS, B_V_COLS, D_BRANCH,
                D_BRANCH, D_BRANCH)
D_IN = 2 * A_QK_COLS + A_V_COLS + 2 * B_QK_COLS + B_V_COLS + 4 * D_BRANCH

kernel_name = "hybrid_diffattn_retention_pool_encoder"


def rmsnorm(x, w):
    xf = x.astype(jnp.float32)
    y = xf * lax.rsqrt(jnp.mean(xf * xf, axis=-1, keepdims=True) + EPS)
    return (y * w.astype(jnp.float32)).astype(x.dtype)


def t5_bucket(rel):
    half = NUM_BUCKETS // 2
    max_exact = half // 2
    ret = jnp.where(rel > 0, half, 0)
    n = jnp.abs(rel)
    nf = jnp.maximum(n, 1).astype(jnp.float32)
    large = max_exact + (jnp.log(nf / max_exact) / math.log(MAX_DISTANCE / max_exact)
                         * (half - max_exact)).astype(jnp.int32)
    large = jnp.minimum(large, half - 1)
    return ret + jnp.where(n < max_exact, n, large)


def diff_attention(q, k, v, rel_bias, lam, lam_init, subln_w):
    B_, S_ = q.shape[0], q.shape[1]
    nb = S_ // Q_BLOCK
    scale = A_QK_DIM ** -0.5
    k_pos = jnp.arange(S_, dtype=jnp.int32)
    qb = q.reshape(B_, nb, Q_BLOCK, A_HEADS, 2, A_QK_DIM).transpose(1, 0, 2, 3, 4, 5)
    bias_table = rel_bias.astype(jnp.float32)

    def block(args):
        q_blk, i = args
        q_pos = i * Q_BLOCK + jnp.arange(Q_BLOCK, dtype=jnp.int32)
        bias = bias_table[t5_bucket(k_pos[None, :] - q_pos[:, None])]
        bias = bias.transpose(2, 0, 1)[None, :, None]
        logits = jnp.einsum('bqhmd,bkhmd->bhmqk', q_blk, k).astype(jnp.float32) * scale + bias
        p = jax.nn.softmax(logits, axis=-1)
        attn = p[:, :, 0] - lam * p[:, :, 1]
        return jnp.einsum('bhqk,bkhd->bqhd', attn.astype(v.dtype), v)

    o = lax.map(block, (qb, jnp.arange(nb, dtype=jnp.int32)))
    o = o.transpose(1, 0, 2, 3, 4).reshape(B_, S_, A_HEADS, A_V_DIM)
    o = rmsnorm(o, subln_w) * (1.0 - lam_init)
    return o.reshape(B_, S_, A_V_COLS)


def rotary(t):
    S_, d = t.shape[1], t.shape[-1]
    half = d // 2
    theta = 1.0 / (ROPE_BASE ** jnp.linspace(0.0, 1.0, half, dtype=jnp.float32))
    ang = jnp.arange(S_, dtype=jnp.float32)[:, None] * theta[None, :]
    cos = jnp.cos(ang)[None, :, None, :]
    sin = jnp.sin(ang)[None, :, None, :]
    t1, t2 = t[..., :half], t[..., half:]
    return jnp.concatenate([t1 * cos - t2 * sin, t1 * sin + t2 * cos], axis=-1)


def retention_dir(q, k, v, log_gamma, strict):
    B_, H, S_, dk = q.shape
    dv = v.shape[-1]
    C = RET_CHUNK
    nc = S_ // C
    idx = jnp.arange(C, dtype=jnp.float32)
    diff = idx[:, None] - idx[None, :]
    mask = (diff > 0) if strict else (diff >= 0)
    decay_intra = jnp.where(mask[None], jnp.exp(log_gamma[:, None, None] * jnp.maximum(diff, 0.0)[None]), 0.0)
    q_dec = jnp.exp(log_gamma[:, None] * (idx + 1.0)[None])[:, :, None]
    k_dec = jnp.exp(log_gamma[:, None] * (C - 1.0 - idx)[None])[:, :, None]
    chunk_dec = jnp.exp(log_gamma * C)[:, None, None]
    qc = q.reshape(B_, H, nc, C, dk).transpose(2, 0, 1, 3, 4)
    kc = k.reshape(B_, H, nc, C, dk).transpose(2, 0, 1, 3, 4)
    vc = v.reshape(B_, H, nc, C, dv).transpose(2, 0, 1, 3, 4)

    def step(R, inp):
        qi, ki, vi = inp
        inner = jnp.einsum('bhnd,bhmd->bhnm', qi, ki) * decay_intra
        out = (jnp.einsum('bhnm,bhmv->bhnv', inner, vi)
               + jnp.einsum('bhnd,bhdv->bhnv', qi * q_dec, R))
        R = R * chunk_dec + jnp.einsum('bhmd,bhmv->bhdv', ki * k_dec, vi)
        return R, out

    R0 = jnp.zeros((B_, H, dk, dv), jnp.float32)
    _, o = lax.scan(step, R0, (qc, kc, vc))
    return o.transpose(1, 2, 0, 3, 4).reshape(B_, H, S_, dv)


def retention(q, k, v, decay_logit):
    B_, S_ = q.shape[0], q.shape[1]
    qf = rotary(q.astype(jnp.float32))
    kf = rotary(k.astype(jnp.float32)) * (B_QK_DIM ** -0.5)
    qf = qf.transpose(0, 2, 1, 3)
    kf = kf.transpose(0, 2, 1, 3)
    vf = v.astype(jnp.float32).transpose(0, 2, 1, 3)
    log_gamma = jax.nn.log_sigmoid(decay_logit.astype(jnp.float32))
    fwd = retention_dir(qf, kf, vf, log_gamma[0], False)
    bwd = retention_dir(jnp.flip(qf, 2), jnp.flip(kf, 2), jnp.flip(vf, 2), log_gamma[1], True)
    o = fwd + jnp.flip(bwd, 2)
    o = o * lax.rsqrt(jnp.mean(o * o, axis=-1, keepdims=True) + EPS)
    return o.transpose(0, 2, 1, 3).reshape(B_, S_, B_V_COLS).astype(v.dtype)


def multiscale_pool(u, pool_w, pool_scale):
    B_, S_ = u.shape[0], u.shape[1]
    uf = u.astype(jnp.float32)
    cs = jnp.concatenate([jnp.zeros_like(uf[:, :1]), jnp.cumsum(uf, axis=1)], axis=1)
    pos = jnp.arange(S_, dtype=jnp.int32)
    outs = []
    for g, w in enumerate(POOL_WINDOWS):
        lo_c, hi_c = g * C_GROUP_DIM, (g + 1) * C_GROUP_DIM
        lo = jnp.clip(pos - w // 2, 0, S_)
        hi = jnp.clip(pos + (w - w // 2), 0, S_)
        window_sum = cs[:, hi, lo_c:hi_c] - cs[:, lo, lo_c:hi_c]
        count = (hi - lo).astype(jnp.float32)[None, :, None]
        outs.append(window_sum / count - uf[:, :, lo_c:hi_c])
    pooled = jnp.stack(outs, axis=2)
    y = jnp.einsum('bsgc,gcd->bsgd', pooled, pool_w.astype(jnp.float32)).reshape(B_, S_, D_BRANCH)
    return (y * pool_scale.astype(jnp.float32)).astype(u.dtype)


def hybrid_layer(x, layer_idx, norm_w, w_in, diff_lambda, diff_subln_w, ret_decay_logit,
                 pool_w, pool_scale, w_out, rel_bias):
    B_, S_ = x.shape[0], x.shape[1]
    h = rmsnorm(x, norm_w)
    proj = jnp.einsum('bsd,de->bse', h, w_in)
    splits = [int(s) for s in np.cumsum(COLUMN_SIZES)[:-1]]
    aq, ak, av, ag, bq, bk, bv, bg, cu, cg = jnp.split(proj, splits, axis=-1)

    lam_init = 0.8 - 0.6 * math.exp(-0.3 * layer_idx)
    lf = diff_lambda.astype(jnp.float32)
    lam = jnp.exp(jnp.sum(lf[0] * lf[1])) - jnp.exp(jnp.sum(lf[2] * lf[3])) + lam_init
    o_a = diff_attention(aq.reshape(B_, S_, A_HEADS, 2, A_QK_DIM),
                         ak.reshape(B_, S_, A_HEADS, 2, A_QK_DIM),
                         av.reshape(B_, S_, A_HEADS, A_V_DIM),
                         rel_bias, lam, lam_init, diff_subln_w)
    o_a = o_a * jax.nn.silu(ag)

    o_b = retention(bq.reshape(B_, S_, B_HEADS, B_QK_DIM),
                    bk.reshape(B_, S_, B_HEADS, B_QK_DIM),
                    bv.reshape(B_, S_, B_HEADS, B_V_DIM),
                    ret_decay_logit)
    o_b = o_b * jax.nn.silu(bg)

    o_c = multiscale_pool(cu, pool_w, pool_scale) * jax.nn.silu(cg)

    mixed = jnp.concatenate([o_a, o_b, o_c], axis=-1)
    return x + jnp.einsum('bse,ed->bsd', mixed, w_out)


def setup_inputs(seed: int = 0) -> dict:
    key = jax.random.key(seed)
    ks = jax.random.split(key, 11)
    x = jax.random.normal(ks[0], (BATCH, SEQ, D_MODEL), jnp.float32)
    norm_w = 1.0 + 0.05 * jax.random.normal(ks[1], (DEPTH, D_MODEL), jnp.float32)
    w_in = jax.random.normal(ks[2], (DEPTH, D_MODEL, D_IN), jnp.float32) * D_MODEL ** -0.5
    diff_lambda = 0.1 * jax.random.normal(ks[3], (DEPTH, 4, A_QK_DIM), jnp.float32)
    diff_subln_w = 1.0 + 0.05 * jax.random.normal(ks[4], (DEPTH, A_V_DIM), jnp.float32)
    e = jnp.stack([5.0 + jnp.arange(B_HEADS, dtype=jnp.float32),
                   5.5 + jnp.arange(B_HEADS, dtype=jnp.float32)])
    ret_decay_logit = (jnp.log(jnp.power(2.0, e) - 1.0)[None]
                       + 0.05 * jax.random.normal(ks[5], (DEPTH, 2, B_HEADS), jnp.float32))
    pool_w = jax.random.normal(ks[6], (DEPTH, C_GROUPS, C_GROUP_DIM, C_GROUP_DIM), jnp.float32) * C_GROUP_DIM ** -0.5
    pool_scale = 0.5 + 0.1 * jax.random.normal(ks[7], (DEPTH, D_BRANCH), jnp.float32)
    w_out = jax.random.normal(ks[8], (DEPTH, D_MIX, D_MODEL), jnp.float32) * D_MIX ** -0.5
    rel_bias = 0.1 * jax.random.normal(ks[9], (NUM_BUCKETS, A_HEADS), jnp.float32)
    final_norm_w = 1.0 + 0.05 * jax.random.normal(ks[10], (D_MODEL,), jnp.float32)
    return {'x': x, 'norm_w': norm_w, 'w_in': w_in, 'diff_lambda': diff_lambda,
            'diff_subln_w': diff_subln_w, 'ret_decay_logit': ret_decay_logit,
            'pool_w': pool_w, 'pool_scale': pool_scale, 'w_out': w_out,
            'rel_bias': rel_bias, 'final_norm_w': final_norm_w}


def reference(x, norm_w, w_in, diff_lambda, diff_subln_w, ret_decay_logit,
              pool_w, pool_scale, w_out, rel_bias, final_norm_w):
    h = x
    for l in range(DEPTH):
        h = hybrid_layer(h, l, norm_w[l], w_in[l], diff_lambda[l], diff_subln_w[l],
                         ret_decay_logit[l], pool_w[l], pool_scale[l], w_out[l], rel_bias)
    return rmsnorm(h, final_norm_w)
```

```python
import functools
import math

import jax
import jax.numpy as jnp
from jax import lax
from jax.experimental import pallas as pl
from jax.experimental.pallas import tpu as pltpu

F32 = jnp.float32
BF16 = jnp.bfloat16

D_MODEL = 1024
DEPTH = 2
D_BRANCH = 512
HEADS = 4
QK_DIM = 64
V_DIM = 128
RET_CHUNK = 128
ROPE_BASE = 10000.0
POOL_WINDOWS = (2, 4, 8, 16)
POOL_HALO = 8
NUM_BUCKETS = 32
MAX_DISTANCE = 128
EPS = 1e-6
LOG2E = math.log2(math.e)

_SIZES = (512, 512, 512, 512, 256, 256, 512, 512, 512, 512)
_OFFS = tuple(sum(_SIZES[:i]) for i in range(len(_SIZES)))
D_IN = sum(_SIZES)

LANES = 128
VMEM_LIMIT = 56 * 1024 * 1024

PROJ_ROWS = 512
ATTN_Q = 256
ATTN_K = 256


def _silu(x):
    return x * jax.nn.sigmoid(x)


def _proj_kernel(x_ref, nw_ref, w_ref, cos_ref, sin_ref,
                 aq_ref, ak_ref, av_ref, ga_ref, bq_ref, bk_ref, bv_ref, gb_ref, cu_ref, gc_ref):
    x = x_ref[...]
    ms = jnp.mean(x * x, axis=-1, keepdims=True)
    h = (x * lax.rsqrt(ms + EPS) * nw_ref[...]).astype(BF16)

    def proj(idx):
        lo = _OFFS[idx]
        return jnp.dot(h, w_ref[:, lo:lo + _SIZES[idx]], preferred_element_type=F32)

    aq_ref[...] = (proj(0) * (QK_DIM ** -0.5 * LOG2E)).astype(BF16)
    ak_ref[...] = proj(1).astype(BF16)
    av_ref[...] = proj(2).astype(BF16)
    ga_ref[...] = _silu(proj(3)).astype(BF16)

    cos = cos_ref[...]
    sin = sin_ref[...]
    lane = lax.broadcasted_iota(jnp.int32, cos.shape, 1)
    first_half = (lane % QK_DIM) < (QK_DIM // 2)
    low_head = lane < QK_DIM

    def rotary_dup(t, scale):
        outs = []
        for half in range(2):
            th = t[:, half * LANES:(half + 1) * LANES]
            swapped = jnp.where(first_half, pltpu.roll(th, LANES - QK_DIM // 2, 1),
                                pltpu.roll(th, QK_DIM // 2, 1))
            r = (th * cos + swapped * sin) * scale
            rr = pltpu.roll(r, QK_DIM, 1)
            outs.append(jnp.where(low_head, r, rr))
            outs.append(jnp.where(low_head, rr, r))
        return jnp.concatenate(outs, axis=1).astype(BF16)

    bq_ref[...] = rotary_dup(proj(4), 1.0)
    bk_ref[...] = rotary_dup(proj(5), QK_DIM ** -0.5)
    bv_ref[...] = proj(6).astype(BF16)
    gb_ref[...] = _silu(proj(7)).astype(BF16)
    cu_ref[...] = proj(8)
    gc_ref[...] = _silu(proj(9)).astype(BF16)


def _proj_call(x2, norm_w, w_in, cos_t, sin_t, seq):
    rows = x2.shape[0]
    tm = PROJ_ROWS
    tiles_per_seq = seq // tm
    row_spec = lambda n: pl.BlockSpec((tm, n), lambda i: (i, 0))
    out_shapes = [jax.ShapeDtypeStruct((rows, D_BRANCH), BF16)] * 8
    out_shapes.insert(8, jax.ShapeDtypeStruct((rows, D_BRANCH), F32))
    out_shapes.append(jax.ShapeDtypeStruct((rows, D_BRANCH), BF16))
    return pl.pallas_call(
        _proj_kernel,
        grid=(rows // tm,),
        in_specs=[
            row_spec(D_MODEL),
            pl.BlockSpec((1, D_MODEL), lambda i: (0, 0)),
            pl.BlockSpec((D_MODEL, D_IN), lambda i: (0, 0)),
            pl.BlockSpec((tm, LANES), lambda i: (i % tiles_per_seq, 0)),
            pl.BlockSpec((tm, LANES), lambda i: (i % tiles_per_seq, 0)),
        ],
        out_specs=[row_spec(D_BRANCH)] * 10,
        out_shape=out_shapes,
        compiler_params=pltpu.CompilerParams(
            dimension_semantics=("arbitrary",), vmem_limit_bytes=VMEM_LIMIT),
        name="proj",
    )(x2, norm_w, w_in, cos_t, sin_t)


def _attn_kernel(lam_ref, q_ref, k_ref, v_ref, g_ref, bias_ref, sw_ref, o_ref,
                 s_scr, vext_scr, *, seq, lam_init):
    tq = ATTN_Q
    qi = pl.program_id(2)
    nkb = seq // ATTN_K

    @pl.when(qi == 0)
    def _():
        vext_scr[:, :V_DIM] = v_ref[0]
        vext_scr[:, V_DIM:] = jnp.ones((seq, V_DIM), BF16)

    q = q_ref[0]
    lane = lax.broadcasted_iota(jnp.int32, q.shape, 1)
    zero = jnp.zeros_like(q)
    q_st = jnp.concatenate([jnp.where(lane < QK_DIM, q, zero),
                            jnp.where(lane < QK_DIM, zero, q)], axis=0)

    m_acc = None
    for c in range(nkb):
        kc = k_ref[0, c * ATTN_K:(c + 1) * ATTN_K, :]
        s = lax.dot_general(q_st, kc, (((1,), (1,)), ((), ())), preferred_element_type=F32)
        b = bias_ref[0, c - qi + (nkb - 1)]
        s = s + jnp.concatenate([b, b], axis=0)
        s_scr[:, c * ATTN_K:(c + 1) * ATTN_K] = s
        bm = jnp.maximum(s[:, :LANES], s[:, LANES:])
        m_acc = bm if m_acc is None else jnp.maximum(m_acc, bm)
    m = jnp.max(m_acc, axis=1, keepdims=True)

    acc = jnp.zeros((2 * tq, 2 * V_DIM), F32)
    for c in range(nkb):
        e = jnp.exp2(s_scr[:, c * ATTN_K:(c + 1) * ATTN_K] - m).astype(BF16)
        acc = acc + jnp.dot(e, vext_scr[c * ATTN_K:(c + 1) * ATTN_K, :],
                            preferred_element_type=F32)
    on = acc[:, :V_DIM] / acc[:, V_DIM:V_DIM + 1]
    o = on[:tq] - lam_ref[0] * on[tq:]
    o = o * lax.rsqrt(jnp.mean(o * o, axis=-1, keepdims=True) + EPS) * sw_ref[...]
    o = o * (1.0 - lam_init)
    o_ref[0] = (o * g_ref[0].astype(F32)).astype(BF16)


def _attn_call(lam, aq, ak, av, ga, bias_t, subln_w, lam_init):
    batch, seq, _ = aq.shape
    tq = ATTN_Q
    nq = seq // tq
    kern = functools.partial(_attn_kernel, seq=seq, lam_init=lam_init)
    q_spec = pl.BlockSpec((1, tq, LANES), lambda b, h, i: (b, i, h))
    kv_spec = pl.BlockSpec((1, seq, LANES), lambda b, h, i: (b, 0, h))
    return pl.pallas_call(
        kern,
        grid=(batch, HEADS, nq),
        in_specs=[
            pl.BlockSpec(memory_space=pltpu.SMEM),
            q_spec, kv_spec, kv_spec, q_spec,
            pl.BlockSpec((1,) + bias_t.shape[1:], lambda b, h, i: (h, 0, 0, 0)),
            pl.BlockSpec((1, V_DIM), lambda b, h, i: (0, 0)),
        ],
        out_specs=q_spec,
        out_shape=jax.ShapeDtypeStruct((batch, seq, D_BRANCH), BF16),
        scratch_shapes=[pltpu.VMEM((2 * tq, seq), F32), pltpu.VMEM((seq, 2 * V_DIM), BF16)],
        compiler_params=pltpu.CompilerParams(
            dimension_semantics=("arbitrary", "arbitrary", "arbitrary"),
            vmem_limit_bytes=VMEM_LIMIT),
        name="diff_attn",
    )(lam, aq, ak, av, ga, bias_t, subln_w)


def _ret_kernel(lg_ref, q_ref, k_ref, v_ref, g_ref, o_ref, kv_scr, r_scr, *, seq):
    C = RET_CHUNK
    nc = seq // C
    hd = pl.program_id(1)
    lgf = lg_ref[0, hd]
    lgb = lg_ref[1, hd]

    ri = lax.broadcasted_iota(jnp.int32, (C, LANES), 0).astype(F32)
    fwd_lane = lax.broadcasted_iota(jnp.int32, (C, LANES), 1) < QK_DIM
    q_scale = jnp.exp(jnp.where(fwd_lane, lgf * (ri + 1.0), lgb * (C - ri)))
    k_scale = jnp.exp(jnp.where(fwd_lane, lgf * (C - 1.0 - ri), lgb * ri))
    dn = (lax.broadcasted_iota(jnp.int32, (C, C), 0)
          - lax.broadcasted_iota(jnp.int32, (C, C), 1)).astype(F32)
    decay = 0.5 * jnp.where(dn >= 0.0, jnp.exp(lgf * jnp.maximum(dn, 0.0)),
                            jnp.exp(lgb * jnp.maximum(-dn, 0.0)))

    for c in range(nc):
        rows = slice(c * C, (c + 1) * C)
        kk = (k_ref[0, rows, :].astype(F32) * k_scale).astype(BF16)
        kv_scr[c] = lax.dot_general(kk, v_ref[0, rows, :], (((0,), (0,)), ((), ())),
                                    preferred_element_type=F32)

    top = lax.broadcasted_iota(jnp.int32, (LANES, V_DIM), 0) < QK_DIM
    dec_f = jnp.exp(lgf * C)
    dec_b = jnp.exp(lgb * C)
    state = jnp.zeros((LANES, V_DIM), F32)
    for c in range(nc):
        r_scr[c] = state
        state = state * dec_f + kv_scr[c]
    state = jnp.zeros((LANES, V_DIM), F32)
    for c in reversed(range(nc)):
        r_scr[c] = jnp.where(top, r_scr[c], state)
        state = state * dec_b + kv_scr[c]

    for c in range(nc):
        rows = slice(c * C, (c + 1) * C)
        qc = q_ref[0, rows, :]
        kc = k_ref[0, rows, :]
        s = lax.dot_general(qc, kc, (((1,), (1,)), ((), ())), preferred_element_type=F32)
        a = (s * decay).astype(BF16)
        qq = (qc.astype(F32) * q_scale).astype(BF16)
        o = (jnp.dot(a, v_ref[0, rows, :], preferred_element_type=F32)
             + jnp.dot(qq, r_scr[c].astype(BF16), preferred_element_type=F32))
        o = o * lax.rsqrt(jnp.mean(o * o, axis=-1, keepdims=True) + EPS)
        o_ref[0, rows, :] = (o * g_ref[0, rows, :].astype(F32)).astype(BF16)


def _ret_call(log_gamma, bq, bk, bv, gb):
    batch, seq, _ = bq.shape
    nc = seq // RET_CHUNK
    spec = pl.BlockSpec((1, seq, LANES), lambda b, h: (b, 0, h))
    return pl.pallas_call(
        functools.partial(_ret_kernel, seq=seq),
        grid=(batch, HEADS),
        in_specs=[pl.BlockSpec(memory_space=pltpu.SMEM), spec, spec, spec, spec],
        out_specs=spec,
        out_shape=jax.ShapeDtypeStruct((batch, seq, D_BRANCH), BF16),
        scratch_shapes=[pltpu.VMEM((nc, LANES, V_DIM), F32), pltpu.VMEM((nc, LANES, V_DIM), F32)],
        compiler_params=pltpu.CompilerParams(
            dimension_semantics=("arbitrary", "arbitrary"), vmem_limit_bytes=VMEM_LIMIT),
        name="retention",
    )(log_gamma, bq, bk, bv, gb)


def _out_kernel(oa_ref, ob_ref, cu_ref, cup_ref, cun_ref, gc_ref, x_ref, wo_ref, pw_ref, ps_ref,
                fnw_ref, o_ref, ext_scr, *, seq, final):
    tm = cu_ref.shape[0]
    tiles_per_seq = seq // tm
    it = pl.program_id(0) % tiles_per_seq

    ext_scr[0:POOL_HALO, :] = jnp.where(it == 0, 0.0, cup_ref[...])
    ext_scr[POOL_HALO:POOL_HALO + tm, :] = cu_ref[...]
    ext_scr[POOL_HALO + tm:, :] = jnp.where(it == tiles_per_seq - 1, 0.0, cun_ref[...])

    pos = it * tm + lax.broadcasted_iota(jnp.int32, (tm, 1), 0)
    ys = []
    for g, w in enumerate(POOL_WINDOWS):
        cols = slice(g * LANES, (g + 1) * LANES)
        wsum = None
        for d in range(-(w // 2), w - w // 2):
            sl = ext_scr[POOL_HALO + d:POOL_HALO + d + tm, cols]
            wsum = sl if wsum is None else wsum + sl
        lo = jnp.maximum(pos - w // 2, 0)
        hi = jnp.minimum(pos + (w - w // 2), seq)
        count = (hi - lo).astype(F32)
        pooled = wsum / count - cu_ref[:, cols]
        ys.append(jnp.dot(pooled.astype(BF16), pw_ref[g], preferred_element_type=F32))
    yc = jnp.concatenate(ys, axis=1) * ps_ref[...] * gc_ref[...].astype(F32)

    mixed = jnp.concatenate([oa_ref[...], ob_ref[...], yc.astype(BF16)], axis=1)
    out = x_ref[...] + jnp.dot(mixed, wo_ref[...], preferred_element_type=F32)
    if final:
        out = out * lax.rsqrt(jnp.mean(out * out, axis=-1, keepdims=True) + EPS) * fnw_ref[...]
    o_ref[...] = out


def _out_call(oa, ob, cu, gc, x2, w_out, pool_w, pool_scale, final_norm_w, seq, final):
    rows = x2.shape[0]
    tm = PROJ_ROWS
    halo_blocks = tm // POOL_HALO
    last_halo = rows // POOL_HALO - 1
    row_spec = lambda n: pl.BlockSpec((tm, n), lambda i: (i, 0))
    const = lambda shape: pl.BlockSpec(shape, lambda i: (0,) * len(shape))
    return pl.pallas_call(
        functools.partial(_out_kernel, seq=seq, final=final),
        grid=(rows // tm,),
        in_specs=[
            row_spec(D_BRANCH), row_spec(D_BRANCH), row_spec(D_BRANCH),
            pl.BlockSpec((POOL_HALO, D_BRANCH),
                         lambda i: (jnp.maximum(i * halo_blocks - 1, 0), 0)),
            pl.BlockSpec((POOL_HALO, D_BRANCH),
                         lambda i: (jnp.minimum((i + 1) * halo_blocks, last_halo), 0)),
            row_spec(D_BRANCH), row_spec(D_MODEL),
            const((3 * D_BRANCH, D_MODEL)),
            const((len(POOL_WINDOWS), LANES, LANES)),
            const((1, D_BRANCH)),
            const((1, D_MODEL)),
        ],
        out_specs=row_spec(D_MODEL),
        out_shape=jax.ShapeDtypeStruct((rows, D_MODEL), F32),
        scratch_shapes=[pltpu.VMEM((tm + 2 * POOL_HALO, D_BRANCH), F32)],
        compiler_params=pltpu.CompilerParams(
            dimension_semantics=("arbitrary",), vmem_limit_bytes=VMEM_LIMIT),
        name="pool_out",
    )(oa, ob, cu, cu, cu, gc, x2, w_out, pool_w, pool_scale, final_norm_w)


def _t5_bucket(rel):
    half = NUM_BUCKETS // 2
    max_exact = half // 2
    ret = jnp.where(rel > 0, half, 0)
    n = jnp.abs(rel)
    nf = jnp.maximum(n, 1).astype(F32)
    large = max_exact + (jnp.log(nf / max_exact) / math.log(MAX_DISTANCE / max_exact)
                         * (half - max_exact)).astype(jnp.int32)
    large = jnp.minimum(large, half - 1)
    return ret + jnp.where(n < max_exact, n, large)


def _bias_tiles(rel_bias, seq):
    t = ATTN_Q
    n = seq // t
    r = jnp.arange(-(n - 1), n, dtype=jnp.int32)[:, None, None] * t
    i = jnp.arange(t, dtype=jnp.int32)[None, :, None]
    j = jnp.arange(t, dtype=jnp.int32)[None, None, :]
    tab = rel_bias.astype(F32)[_t5_bucket(r + j - i)]
    return tab.transpose(3, 0, 1, 2) * LOG2E


def _rotary_tables(seq):
    half = QK_DIM // 2
    theta = 1.0 / (ROPE_BASE ** jnp.linspace(0.0, 1.0, half, dtype=F32))
    ang = jnp.arange(seq, dtype=F32)[:, None] * theta[None, :]
    cos = jnp.cos(ang)
    sin = jnp.sin(ang)
    reps = LANES // QK_DIM
    cos_t = jnp.tile(jnp.concatenate([cos, cos], axis=1), (1, reps))
    sin_t = jnp.tile(jnp.concatenate([-sin, sin], axis=1), (1, reps))
    return cos_t, sin_t


def kernel(x, norm_w, w_in, diff_lambda, diff_subln_w, ret_decay_logit, pool_w, pool_scale, w_out,
           rel_bias, final_norm_w):
    batch, seq, d_model = x.shape
    assert d_model == D_MODEL and seq % PROJ_ROWS == 0 and seq % ATTN_Q == 0
    rows = batch * seq
    cos_t, sin_t = _rotary_tables(seq)
    bias_t = _bias_tiles(rel_bias, seq)
    fnw = final_norm_w.reshape(1, D_MODEL).astype(F32)

    h = x.reshape(rows, D_MODEL)
    for l in range(DEPTH):
        lam_init = 0.8 - 0.6 * math.exp(-0.3 * l)
        lf = diff_lambda[l].astype(F32)
        lam = (jnp.exp(jnp.sum(lf[0] * lf[1])) - jnp.exp(jnp.sum(lf[2] * lf[3])) + lam_init)
        log_gamma = jax.nn.log_sigmoid(ret_decay_logit[l].astype(F32))

        aq, ak, av, ga, bq, bk, bv, gb, cu, gc = _proj_call(
            h, norm_w[l].reshape(1, D_MODEL).astype(F32), w_in[l].astype(BF16), cos_t, sin_t, seq)
        to3 = lambda a: a.reshape(batch, seq, D_BRANCH)
        oa = _attn_call(lam.reshape(1), to3(aq), to3(ak), to3(av), to3(ga), bias_t,
                        diff_subln_w[l].reshape(1, V_DIM).astype(F32), lam_init)
        ob = _ret_call(log_gamma, to3(bq), to3(bk), to3(bv), to3(gb))
        h = _out_call(oa.reshape(rows, D_BRANCH), ob.reshape(rows, D_BRANCH), cu, gc, h,
                      w_out[l].astype(BF16), pool_w[l].astype(BF16),
                      pool_scale[l].reshape(1, D_BRANCH).astype(F32), fnw, seq,
                      final=(l == DEPTH - 1))
    return h.reshape(batch, seq, D_MODEL)
```

```python
import functools
import math

import jax
import jax.numpy as jnp
from jax import lax
from jax.experimental import pallas as pl
from jax.experimental.pallas import tpu as pltpu

F32 = jnp.float32
BF16 = jnp.bfloat16

D_MODEL = 1024
DEPTH = 2
D_BRANCH = 512
HEADS = 4
QK_DIM = 64
V_DIM = 128
RET_CHUNK = 128
ROPE_BASE = 10000.0
POOL_WINDOWS = (2, 4, 8, 16)
POOL_HALO = 8
NUM_BUCKETS = 32
MAX_DISTANCE = 128
EPS = 1e-6
LOG2E = math.log2(math.e)

_SIZES = (512, 512, 512, 512, 256, 256, 512, 512, 512, 512)
_OFFS = tuple(sum(_SIZES[:i]) for i in range(len(_SIZES)))
D_IN = sum(_SIZES)

LANES = 128
VMEM_LIMIT = 56 * 1024 * 1024

PROJ_ROWS = 512
ATTN_Q = 256
ATTN_K = 256


def _silu(x):
    return x * jax.nn.sigmoid(x)


def _proj_kernel(x_ref, nw_ref, w_ref, cos_ref, sin_ref,
                 aq_ref, ak_ref, av_ref, ga_ref, bq_ref, bk_ref, bv_ref, gb_ref, cu_ref, gc_ref):
    x = x_ref[...]
    ms = jnp.mean(x * x, axis=-1, keepdims=True)
    h = (x * lax.rsqrt(ms + EPS) * nw_ref[...]).astype(BF16)

    def proj(idx):
        lo = _OFFS[idx]
        return jnp.dot(h, w_ref[:, lo:lo + _SIZES[idx]], preferred_element_type=F32)

    aq_ref[...] = (proj(0) * (QK_DIM ** -0.5 * LOG2E)).astype(BF16)
    ak_ref[...] = proj(1).astype(BF16)
    av_ref[...] = proj(2).astype(BF16)
    ga_ref[...] = _silu(proj(3)).astype(BF16)

    cos = cos_ref[...]
    sin = sin_ref[...]
    lane = lax.broadcasted_iota(jnp.int32, cos.shape, 1)
    first_half = (lane % QK_DIM) < (QK_DIM // 2)
    low_head = lane < QK_DIM

    def rotary_dup(t, scale):
        outs = []
        for half in range(2):
            th = t[:, half * LANES:(half + 1) * LANES]
            swapped = jnp.where(first_half, pltpu.roll(th, LANES - QK_DIM // 2, 1),
                                pltpu.roll(th, QK_DIM // 2, 1))
            r = (th * cos + swapped * sin) * scale
            rr = pltpu.roll(r, QK_DIM, 1)
            outs.append(jnp.where(low_head, r, rr))
            outs.append(jnp.where(low_head, rr, r))
        return jnp.concatenate(outs, axis=1).astype(BF16)

    bq_ref[...] = rotary_dup(proj(4), 1.0)
    bk_ref[...] = rotary_dup(proj(5), QK_DIM ** -0.5)
    bv_ref[...] = proj(6).astype(BF16)
    gb_ref[...] = _silu(proj(7)).astype(BF16)
    cu_ref[...] = proj(8)
    gc_ref[...] = _silu(proj(9)).astype(BF16)


def _proj_call(x2, norm_w, w_in, cos_t, sin_t, seq):
    rows = x2.shape[0]
    tm = PROJ_ROWS
    tiles_per_seq = seq // tm
    row_spec = lambda n: pl.BlockSpec((tm, n), lambda i: (i, 0))
    out_shapes = [jax.ShapeDtypeStruct((rows, D_BRANCH), BF16)] * 8
    out_shapes.insert(8, jax.ShapeDtypeStruct((rows, D_BRANCH), F32))
    out_shapes.append(jax.ShapeDtypeStruct((rows, D_BRANCH), BF16))
    return pl.pallas_call(
        _proj_kernel,
        grid=(rows // tm,),
        in_specs=[
            row_spec(D_MODEL),
            pl.BlockSpec((1, D_MODEL), lambda i: (0, 0)),
            pl.BlockSpec((D_MODEL, D_IN), lambda i: (0, 0)),
            pl.BlockSpec((tm, LANES), lambda i: (i % tiles_per_seq, 0)),
            pl.BlockSpec((tm, LANES), lambda i: (i % tiles_per_seq, 0)),
        ],
        out_specs=[row_spec(D_BRANCH)] * 10,
        out_shape=out_shapes,
        compiler_params=pltpu.CompilerParams(
            dimension_semantics=("arbitrary",), vmem_limit_bytes=VMEM_LIMIT),
        name="proj",
    )(x2, norm_w, w_in, cos_t, sin_t)


def _attn_kernel(lam_ref, q_ref, k_ref, v_ref, g_ref, bias_ref, sw_ref, o_ref,
                 s_scr, vext_scr, *, seq, lam_init):
    tq = ATTN_Q
    qi = pl.program_id(2)
    nkb = seq // ATTN_K

    @pl.when(qi == 0)
    def _():
        vext_scr[:, :V_DIM] = v_ref[0]
        vext_scr[:, V_DIM:] = jnp.ones((seq, V_DIM), BF16)

    q = q_ref[0]
    lane = lax.broadcasted_iota(jnp.int32, q.shape, 1)
    zero = jnp.zeros_like(q)
    q_st = jnp.concatenate([jnp.where(lane < QK_DIM, q, zero),
                            jnp.where(lane < QK_DIM, zero, q)], axis=0)

    m_acc = None
    for c in range(nkb):
        kc = k_ref[0, c * ATTN_K:(c + 1) * ATTN_K, :]
        s = lax.dot_general(q_st, kc, (((1,), (1,)), ((), ())), preferred_element_type=F32)
        rel = c - qi
        b = bias_ref[0, jnp.where(rel < -1, 3, jnp.where(rel > 1, 4, rel + 1))]
        s = s + jnp.concatenate([b, b], axis=0)
        s_scr[:, c * ATTN_K:(c + 1) * ATTN_K] = s
        bm = jnp.maximum(s[:, :LANES], s[:, LANES:])
        m_acc = bm if m_acc is None else jnp.maximum(m_acc, bm)
    m = jnp.max(m_acc, axis=1, keepdims=True)

    acc = jnp.zeros((2 * tq, 2 * V_DIM), F32)
    for c in range(nkb):
        e = jnp.exp2(s_scr[:, c * ATTN_K:(c + 1) * ATTN_K] - m).astype(BF16)
        acc = acc + jnp.dot(e, vext_scr[c * ATTN_K:(c + 1) * ATTN_K, :],
                            preferred_element_type=F32)
    on = acc[:, :V_DIM] / acc[:, V_DIM:V_DIM + 1]
    o = on[:tq] - lam_ref[0] * on[tq:]
    o = o * lax.rsqrt(jnp.mean(o * o, axis=-1, keepdims=True) + EPS) * sw_ref[...]
    o = o * (1.0 - lam_init)
    o_ref[0] = (o * g_ref[0].astype(F32)).astype(BF16)


def _attn_call(lam, aq, ak, av, ga, bias_t, subln_w, lam_init):
    batch, seq, _ = aq.shape
    tq = ATTN_Q
    nq = seq // tq
    kern = functools.partial(_attn_kernel, seq=seq, lam_init=lam_init)
    q_spec = pl.BlockSpec((1, tq, LANES), lambda b, h, i: (b, i, h))
    kv_spec = pl.BlockSpec((1, seq, LANES), lambda b, h, i: (b, 0, h))
    return pl.pallas_call(
        kern,
        grid=(batch, HEADS, nq),
        in_specs=[
            pl.BlockSpec(memory_space=pltpu.SMEM),
            q_spec, kv_spec, kv_spec, q_spec,
            pl.BlockSpec((1,) + bias_t.shape[1:], lambda b, h, i: (h, 0, 0, 0)),
            pl.BlockSpec((1, V_DIM), lambda b, h, i: (0, 0)),
        ],
        out_specs=q_spec,
        out_shape=jax.ShapeDtypeStruct((batch, seq, D_BRANCH), BF16),
        scratch_shapes=[pltpu.VMEM((2 * tq, seq), F32), pltpu.VMEM((seq, 2 * V_DIM), BF16)],
        compiler_params=pltpu.CompilerParams(
            dimension_semantics=("arbitrary", "arbitrary", "arbitrary"),
            vmem_limit_bytes=VMEM_LIMIT),
        name="diff_attn",
    )(lam, aq, ak, av, ga, bias_t, subln_w)


def _ret_kernel(lg_ref, q_ref, k_ref, v_ref, g_ref, o_ref, kv_scr, r_scr, *, seq):
    C = RET_CHUNK
    nc = seq // C
    hd = pl.program_id(1)
    lgf = lg_ref[0, hd]
    lgb = lg_ref[1, hd]

    ri = lax.broadcasted_iota(jnp.int32, (C, LANES), 0).astype(F32)
    fwd_lane = lax.broadcasted_iota(jnp.int32, (C, LANES), 1) < QK_DIM
    q_scale = jnp.exp(jnp.where(fwd_lane, lgf * (ri + 1.0), lgb * (C - ri)))
    k_scale = jnp.exp(jnp.where(fwd_lane, lgf * (C - 1.0 - ri), lgb * ri))
    dn = (lax.broadcasted_iota(jnp.int32, (C, C), 0)
          - lax.broadcasted_iota(jnp.int32, (C, C), 1)).astype(F32)
    decay = 0.5 * jnp.where(dn >= 0.0, jnp.exp(lgf * jnp.maximum(dn, 0.0)),
                            jnp.exp(lgb * jnp.maximum(-dn, 0.0)))

    for c in range(nc):
        rows = slice(c * C, (c + 1) * C)
        kk = (k_ref[0, rows, :].astype(F32) * k_scale).astype(BF16)
        kv_scr[c] = lax.dot_general(kk, v_ref[0, rows, :], (((0,), (0,)), ((), ())),
                                    preferred_element_type=F32)

    top = lax.broadcasted_iota(jnp.int32, (LANES, V_DIM), 0) < QK_DIM
    dec_f = jnp.exp(lgf * C)
    dec_b = jnp.exp(lgb * C)
    state = jnp.zeros((LANES, V_DIM), F32)
    for c in range(nc):
        r_scr[c] = state
        state = state * dec_f + kv_scr[c]
    state = jnp.zeros((LANES, V_DIM), F32)
    for c in reversed(range(nc)):
        r_scr[c] = jnp.where(top, r_scr[c], state)
        state = state * dec_b + kv_scr[c]

    for c in range(nc):
        rows = slice(c * C, (c + 1) * C)
        qc = q_ref[0, rows, :]
        kc = k_ref[0, rows, :]
        s = lax.dot_general(qc, kc, (((1,), (1,)), ((), ())), preferred_element_type=F32)
        a = (s * decay).astype(BF16)
        qq = (qc.astype(F32) * q_scale).astype(BF16)
        o = (jnp.dot(a, v_ref[0, rows, :], preferred_element_type=F32)
             + jnp.dot(qq, r_scr[c].astype(BF16), preferred_element_type=F32))
        o = o * lax.rsqrt(jnp.mean(o * o, axis=-1, keepdims=True) + EPS)
        o_ref[0, rows, :] = (o * g_ref[0, rows, :].astype(F32)).astype(BF16)


def _ret_call(log_gamma, bq, bk, bv, gb):
    batch, seq, _ = bq.shape
    nc = seq // RET_CHUNK
    spec = pl.BlockSpec((1, seq, LANES), lambda b, h: (b, 0, h))
    return pl.pallas_call(
        functools.partial(_ret_kernel, seq=seq),
        grid=(batch, HEADS),
        in_specs=[pl.BlockSpec(memory_space=pltpu.SMEM), spec, spec, spec, spec],
        out_specs=spec,
        out_shape=jax.ShapeDtypeStruct((batch, seq, D_BRANCH), BF16),
        scratch_shapes=[pltpu.VMEM((nc, LANES, V_DIM), F32), pltpu.VMEM((nc, LANES, V_DIM), F32)],
        compiler_params=pltpu.CompilerParams(
            dimension_semantics=("arbitrary", "arbitrary"), vmem_limit_bytes=VMEM_LIMIT),
        name="retention",
    )(log_gamma, bq, bk, bv, gb)


def _out_kernel(oa_ref, ob_ref, cu_ref, cup_ref, cun_ref, gc_ref, x_ref, wo_ref, pw_ref, ps_ref,
                fnw_ref, o_ref, ext_scr, *, seq, final):
    tm = cu_ref.shape[0]
    tiles_per_seq = seq // tm
    it = pl.program_id(0) % tiles_per_seq

    ext_scr[0:POOL_HALO, :] = jnp.where(it == 0, 0.0, cup_ref[...])
    ext_scr[POOL_HALO:POOL_HALO + tm, :] = cu_ref[...]
    ext_scr[POOL_HALO + tm:, :] = jnp.where(it == tiles_per_seq - 1, 0.0, cun_ref[...])

    pos = it * tm + lax.broadcasted_iota(jnp.int32, (tm, 1), 0)
    ys = []
    for g, w in enumerate(POOL_WINDOWS):
        cols = slice(g * LANES, (g + 1) * LANES)
        wsum = None
        for d in range(-(w // 2), w - w // 2):
            sl = ext_scr[POOL_HALO + d:POOL_HALO + d + tm, cols]
            wsum = sl if wsum is None else wsum + sl
        lo = jnp.maximum(pos - w // 2, 0)
        hi = jnp.minimum(pos + (w - w // 2), seq)
        count = (hi - lo).astype(F32)
        pooled = wsum / count - cu_ref[:, cols]
        ys.append(jnp.dot(pooled.astype(BF16), pw_ref[g], preferred_element_type=F32))
    yc = jnp.concatenate(ys, axis=1) * ps_ref[...] * gc_ref[...].astype(F32)

    mixed = jnp.concatenate([oa_ref[...], ob_ref[...], yc.astype(BF16)], axis=1)
    out = x_ref[...] + jnp.dot(mixed, wo_ref[...], preferred_element_type=F32)
    if final:
        out = out * lax.rsqrt(jnp.mean(out * out, axis=-1, keepdims=True) + EPS) * fnw_ref[...]
    o_ref[...] = out


def _out_call(oa, ob, cu, gc, x2, w_out, pool_w, pool_scale, final_norm_w, seq, final):
    rows = x2.shape[0]
    tm = PROJ_ROWS
    halo_blocks = tm // POOL_HALO
    last_halo = rows // POOL_HALO - 1
    row_spec = lambda n: pl.BlockSpec((tm, n), lambda i: (i, 0))
    const = lambda shape: pl.BlockSpec(shape, lambda i: (0,) * len(shape))
    return pl.pallas_call(
        functools.partial(_out_kernel, seq=seq, final=final),
        grid=(rows // tm,),
        in_specs=[
            row_spec(D_BRANCH), row_spec(D_BRANCH), row_spec(D_BRANCH),
            pl.BlockSpec((POOL_HALO, D_BRANCH),
                         lambda i: (jnp.maximum(i * halo_blocks - 1, 0), 0)),
            pl.BlockSpec((POOL_HALO, D_BRANCH),
                         lambda i: (jnp.minimum((i + 1) * halo_blocks, last_halo), 0)),
            row_spec(D_BRANCH), row_spec(D_MODEL),
            const((3 * D_BRANCH, D_MODEL)),
            const((len(POOL_WINDOWS), LANES, LANES)),
            const((1, D_BRANCH)),
            const((1, D_MODEL)),
        ],
        out_specs=row_spec(D_MODEL),
        out_shape=jax.ShapeDtypeStruct((rows, D_MODEL), F32),
        scratch_shapes=[pltpu.VMEM((tm + 2 * POOL_HALO, D_BRANCH), F32)],
        compiler_params=pltpu.CompilerParams(
            dimension_semantics=("arbitrary",), vmem_limit_bytes=VMEM_LIMIT),
        name="pool_out",
    )(oa, ob, cu, cu, cu, gc, x2, w_out, pool_w, pool_scale, final_norm_w)


def _t5_bucket(rel):
    half = NUM_BUCKETS // 2
    max_exact = half // 2
    ret = jnp.where(rel > 0, half, 0)
    n = jnp.abs(rel)
    nf = jnp.maximum(n, 1).astype(F32)
    large = max_exact + (jnp.log(nf / max_exact) / math.log(MAX_DISTANCE / max_exact)
                         * (half - max_exact)).astype(jnp.int32)
    large = jnp.minimum(large, half - 1)
    return ret + jnp.where(n < max_exact, n, large)


def _bias_kernel(u_ref, o_ref):
    t = ATTN_Q
    for r in range(u_ref.shape[1]):
        x = jnp.broadcast_to(u_ref[0, r:r + 1, :], (t, 2 * t))
        o_ref[0, r] = pltpu.roll(x, t + 1, 1, stride=1, stride_axis=0)[:, :t]


def _bias_tiles(rel_bias, seq):
    t = ATTN_Q
    assert t >= MAX_DISTANCE and ATTN_K == t
    off = jnp.arange(2 * t, dtype=jnp.int32) - (t - 1)
    rel = jnp.stack([off - t, off, off + t, jnp.full_like(off, -seq), jnp.full_like(off, seq)])
    u = rel_bias.astype(F32)[_t5_bucket(rel)].transpose(2, 0, 1) * LOG2E
    n_tiles = rel.shape[0]
    return pl.pallas_call(
        _bias_kernel,
        grid=(HEADS,),
        in_specs=[pl.BlockSpec((1, n_tiles, 2 * t), lambda h: (h, 0, 0))],
        out_specs=pl.BlockSpec((1, n_tiles, t, t), lambda h: (h, 0, 0, 0)),
        out_shape=jax.ShapeDtypeStruct((HEADS, n_tiles, t, t), F32),
        compiler_params=pltpu.CompilerParams(dimension_semantics=("arbitrary",)),
        name="bias_tiles",
    )(u)


def _rotary_tables(seq):
    half = QK_DIM // 2
    theta = 1.0 / (ROPE_BASE ** jnp.linspace(0.0, 1.0, half, dtype=F32))
    ang = jnp.arange(seq, dtype=F32)[:, None] * theta[None, :]
    cos = jnp.cos(ang)
    sin = jnp.sin(ang)
    reps = LANES // QK_DIM
    cos_t = jnp.tile(jnp.concatenate([cos, cos], axis=1), (1, reps))
    sin_t = jnp.tile(jnp.concatenate([-sin, sin], axis=1), (1, reps))
    return cos_t, sin_t


def kernel(x, norm_w, w_in, diff_lambda, diff_subln_w, ret_decay_logit, pool_w, pool_scale, w_out,
           rel_bias, final_norm_w):
    batch, seq, d_model = x.shape
    assert d_model == D_MODEL and seq % PROJ_ROWS == 0 and seq % ATTN_Q == 0
    rows = batch * seq
    cos_t, sin_t = _rotary_tables(seq)
    bias_t = _bias_tiles(rel_bias, seq)
    fnw = final_norm_w.reshape(1, D_MODEL).astype(F32)

    h = x.reshape(rows, D_MODEL)
    for l in range(DEPTH):
        lam_init = 0.8 - 0.6 * math.exp(-0.3 * l)
        lf = diff_lambda[l].astype(F32)
        lam = (jnp.exp(jnp.sum(lf[0] * lf[1])) - jnp.exp(jnp.sum(lf[2] * lf[3])) + lam_init)
        log_gamma = jax.nn.log_sigmoid(ret_decay_logit[l].astype(F32))

        aq, ak, av, ga, bq, bk, bv, gb, cu, gc = _proj_call(
            h, norm_w[l].reshape(1, D_MODEL).astype(F32), w_in[l].astype(BF16), cos_t, sin_t, seq)
        to3 = lambda a: a.reshape(batch, seq, D_BRANCH)
        oa = _attn_call(lam.reshape(1), to3(aq), to3(ak), to3(av), to3(ga), bias_t,
                        diff_subln_w[l].reshape(1, V_DIM).astype(F32), lam_init)
        ob = _ret_call(log_gamma, to3(bq), to3(bk), to3(bv), to3(gb))
        h = _out_call(oa.reshape(rows, D_BRANCH), ob.reshape(rows, D_BRANCH), cu, gc, h,
                      w_out[l].astype(BF16), pool_w[l].astype(BF16),
                      pool_scale[l].reshape(1, D_BRANCH).astype(F32), fnw, seq,
                      final=(l == DEPTH - 1))
    return h.reshape(batch, seq, D_MODEL)
```

```python
import functools
import math

import jax
import jax.numpy as jnp
from jax import lax
from jax.experimental import pallas as pl
from jax.experimental.pallas import tpu as pltpu

F32 = jnp.float32
BF16 = jnp.bfloat16

D_MODEL = 1024
DEPTH = 2
D_BRANCH = 512
HEADS = 4
QK_DIM = 64
V_DIM = 128
RET_CHUNK = 128
ROPE_BASE = 10000.0
POOL_WINDOWS = (2, 4, 8, 16)
POOL_HALO = 8
NUM_BUCKETS = 32
MAX_DISTANCE = 128
EPS = 1e-6
LOG2E = math.log2(math.e)

_SIZES = (512, 512, 512, 512, 256, 256, 512, 512, 512, 512)
_OFFS = tuple(sum(_SIZES[:i]) for i in range(len(_SIZES)))
D_IN = sum(_SIZES)

LANES = 128
VMEM_LIMIT = 56 * 1024 * 1024

PROJ_ROWS = 512
ATTN_Q = 256
ATTN_K = 256


def _silu(x):
    return x * jax.nn.sigmoid(x)


def _proj_kernel(x_ref, nw_ref, w_ref, cos_ref, sin_ref,
                 aq_ref, ak_ref, av_ref, ga_ref, bq_ref, bk_ref, bv_ref, gb_ref, cu_ref, gc_ref):
    x = x_ref[...]
    ms = jnp.mean(x * x, axis=-1, keepdims=True)
    h = (x * lax.rsqrt(ms + EPS) * nw_ref[...]).astype(BF16)

    def proj(idx):
        lo = _OFFS[idx]
        return jnp.dot(h, w_ref[:, lo:lo + _SIZES[idx]], preferred_element_type=F32)

    aq_ref[...] = (proj(0) * (QK_DIM ** -0.5 * LOG2E)).astype(BF16)
    ak_ref[...] = proj(1).astype(BF16)
    av = proj(2).astype(BF16)
    ones = jnp.ones((av.shape[0], V_DIM), BF16)
    av_ref[...] = jnp.concatenate(
        [blk for hd in range(HEADS) for blk in (av[:, hd * V_DIM:(hd + 1) * V_DIM], ones)], axis=1)
    ga_ref[...] = _silu(proj(3)).astype(BF16)

    cos = cos_ref[...]
    sin = sin_ref[...]
    lane = lax.broadcasted_iota(jnp.int32, cos.shape, 1)
    first_half = (lane % QK_DIM) < (QK_DIM // 2)
    low_head = lane < QK_DIM

    def rotary_dup(t, scale):
        outs = []
        for half in range(2):
            th = t[:, half * LANES:(half + 1) * LANES]
            swapped = jnp.where(first_half, pltpu.roll(th, LANES - QK_DIM // 2, 1),
                                pltpu.roll(th, QK_DIM // 2, 1))
            r = (th * cos + swapped * sin) * scale
            rr = pltpu.roll(r, QK_DIM, 1)
            outs.append(jnp.where(low_head, r, rr))
            outs.append(jnp.where(low_head, rr, r))
        return jnp.concatenate(outs, axis=1).astype(BF16)

    bq_ref[...] = rotary_dup(proj(4), 1.0)
    bk_ref[...] = rotary_dup(proj(5), QK_DIM ** -0.5)
    bv_ref[...] = proj(6).astype(BF16)
    gb_ref[...] = _silu(proj(7)).astype(BF16)
    cu_ref[...] = proj(8)
    gc_ref[...] = _silu(proj(9)).astype(BF16)


def _proj_call(x2, norm_w, w_in, cos_t, sin_t, seq):
    rows = x2.shape[0]
    tm = PROJ_ROWS
    tiles_per_seq = seq // tm
    row_spec = lambda n: pl.BlockSpec((tm, n), lambda i: (i, 0))
    widths = [D_BRANCH] * 10
    widths[2] = 2 * D_BRANCH
    dtypes = [BF16] * 10
    dtypes[8] = F32
    out_shapes = [jax.ShapeDtypeStruct((rows, w), d) for w, d in zip(widths, dtypes)]
    return pl.pallas_call(
        _proj_kernel,
        grid=(rows // tm,),
        in_specs=[
            row_spec(D_MODEL),
            pl.BlockSpec((1, D_MODEL), lambda i: (0, 0)),
            pl.BlockSpec((D_MODEL, D_IN), lambda i: (0, 0)),
            pl.BlockSpec((tm, LANES), lambda i: (i % tiles_per_seq, 0)),
            pl.BlockSpec((tm, LANES), lambda i: (i % tiles_per_seq, 0)),
        ],
        out_specs=[row_spec(w) for w in widths],
        out_shape=out_shapes,
        compiler_params=pltpu.CompilerParams(
            dimension_semantics=("arbitrary",), vmem_limit_bytes=VMEM_LIMIT),
        name="proj",
    )(x2, norm_w, w_in, cos_t, sin_t)


def _attn_kernel(lam_ref, q_ref, k_ref, v_ref, g_ref, bias_ref, sw_ref, o_ref,
                 s_scr, m_scr, *, seq, nq, lam_init):
    tq = ATTN_Q
    j = pl.program_id(0)
    nkb = seq // ATTN_K

    @pl.when(j == 0)
    def _():
        s_scr[...] = jnp.zeros(s_scr.shape, F32)
        m_scr[...] = jnp.zeros(m_scr.shape, F32)

    qi = jnp.minimum(j, pl.num_programs(0) - 2) % nq
    q = q_ref[0]
    lane = lax.broadcasted_iota(jnp.int32, q.shape, 1)
    zero = jnp.zeros_like(q)
    q_st = jnp.concatenate([jnp.where(lane < QK_DIM, q, zero),
                            jnp.where(lane < QK_DIM, zero, q)], axis=0)

    m_b = m_scr[...]
    acc = jnp.zeros((2 * tq, 2 * V_DIM), F32)
    m_acc = None
    for c in range(nkb):
        cols = slice(c * ATTN_K, (c + 1) * ATTN_K)
        sb = s_scr[:, cols]
        e = jnp.exp2(jnp.concatenate([sb[:, :LANES] - m_b, sb[:, LANES:] - m_b], axis=1))
        acc = acc + jnp.dot(e.astype(BF16), v_ref[0, cols, :], preferred_element_type=F32)
        s = lax.dot_general(q_st, k_ref[0, cols, :], (((1,), (1,)), ((), ())),
                            preferred_element_type=F32)
        rel = c - qi
        b = bias_ref[0, jnp.where(rel < -1, 3, jnp.where(rel > 1, 4, rel + 1))]
        s = s + jnp.concatenate([b, b], axis=0)
        s_scr[:, cols] = s
        bm = jnp.maximum(s[:, :LANES], s[:, LANES:])
        m_acc = bm if m_acc is None else jnp.maximum(m_acc, bm)
    m_new = jnp.max(m_acc, axis=1, keepdims=True)
    m_scr[...] = jnp.broadcast_to(m_new, m_acc.shape)

    on = acc[:, :V_DIM] / acc[:, V_DIM:]
    o = on[:tq] - lam_ref[0] * on[tq:]
    o = o * lax.rsqrt(jnp.mean(o * o, axis=-1, keepdims=True) + EPS) * sw_ref[...]
    o = o * (1.0 - lam_init)
    o_ref[0] = (o * g_ref[0].astype(F32)).astype(BF16)


def _attn_call(lam, aq, ak, av, ga, bias_t, subln_w, lam_init):
    batch, seq, _ = aq.shape
    tq = ATTN_Q
    nq = seq // tq
    n_tiles = batch * HEADS * nq
    kern = functools.partial(_attn_kernel, seq=seq, nq=nq, lam_init=lam_init)

    def front(j):
        t = jnp.minimum(j, n_tiles - 1)
        return t // (HEADS * nq), (t // nq) % HEADS, t % nq

    def back(j):
        t = jnp.maximum(j - 1, 0)
        return t // (HEADS * nq), (t // nq) % HEADS, t % nq

    def tile_of(where):
        return lambda j: (where(j)[0], where(j)[2], where(j)[1])

    def seq_of(where):
        return lambda j: (where(j)[0], 0, where(j)[1])

    return pl.pallas_call(
        kern,
        grid=(n_tiles + 1,),
        in_specs=[
            pl.BlockSpec(memory_space=pltpu.SMEM),
            pl.BlockSpec((1, tq, LANES), tile_of(front)),
            pl.BlockSpec((1, seq, LANES), seq_of(front)),
            pl.BlockSpec((1, seq, 2 * V_DIM), seq_of(back)),
            pl.BlockSpec((1, tq, LANES), tile_of(back)),
            pl.BlockSpec((1,) + bias_t.shape[1:], lambda j: (front(j)[1], 0, 0, 0)),
            pl.BlockSpec((1, V_DIM), lambda j: (0, 0)),
        ],
        out_specs=pl.BlockSpec((1, tq, LANES), tile_of(back)),
        out_shape=jax.ShapeDtypeStruct((batch, seq, D_BRANCH), BF16),
        scratch_shapes=[pltpu.VMEM((2 * tq, seq), F32), pltpu.VMEM((2 * tq, LANES), F32)],
        compiler_params=pltpu.CompilerParams(
            dimension_semantics=("arbitrary",), vmem_limit_bytes=VMEM_LIMIT),
        name="diff_attn",
    )(lam, aq, ak, av, ga, bias_t, subln_w)


def _ret_kernel(lg_ref, q_ref, k_ref, v_ref, g_ref, o_ref, kv_scr, r_scr, *, seq):
    C = RET_CHUNK
    nc = seq // C
    hd = pl.program_id(1)
    lgf = lg_ref[0, hd]
    lgb = lg_ref[1, hd]

    ri = lax.broadcasted_iota(jnp.int32, (C, LANES), 0).astype(F32)
    fwd_lane = lax.broadcasted_iota(jnp.int32, (C, LANES), 1) < QK_DIM
    q_scale = jnp.exp(jnp.where(fwd_lane, lgf * (ri + 1.0), lgb * (C - ri)))
    k_scale = jnp.exp(jnp.where(fwd_lane, lgf * (C - 1.0 - ri), lgb * ri))
    dn = (lax.broadcasted_iota(jnp.int32, (C, C), 0)
          - lax.broadcasted_iota(jnp.int32, (C, C), 1)).astype(F32)
    decay = 0.5 * jnp.where(dn >= 0.0, jnp.exp(lgf * jnp.maximum(dn, 0.0)),
                            jnp.exp(lgb * jnp.maximum(-dn, 0.0)))

    for c in range(nc):
        rows = slice(c * C, (c + 1) * C)
        kk = (k_ref[0, rows, :].astype(F32) * k_scale).astype(BF16)
        kv_scr[c] = lax.dot_general(kk, v_ref[0, rows, :], (((0,), (0,)), ((), ())),
                                    preferred_element_type=F32)

    top = lax.broadcasted_iota(jnp.int32, (LANES, V_DIM), 0) < QK_DIM
    dec_f = jnp.exp(lgf * C)
    dec_b = jnp.exp(lgb * C)
    state = jnp.zeros((LANES, V_DIM), F32)
    for c in range(nc):
        r_scr[c] = state
        state = state * dec_f + kv_scr[c]
    state = jnp.zeros((LANES, V_DIM), F32)
    for c in reversed(range(nc)):
        r_scr[c] = jnp.where(top, r_scr[c], state)
        state = state * dec_b + kv_scr[c]

    for c in range(nc):
        rows = slice(c * C, (c + 1) * C)
        qc = q_ref[0, rows, :]
        kc = k_ref[0, rows, :]
        s = lax.dot_general(qc, kc, (((1,), (1,)), ((), ())), preferred_element_type=F32)
        a = (s * decay).astype(BF16)
        qq = (qc.astype(F32) * q_scale).astype(BF16)
        o = (jnp.dot(a, v_ref[0, rows, :], preferred_element_type=F32)
             + jnp.dot(qq, r_scr[c].astype(BF16), preferred_element_type=F32))
        o = o * lax.rsqrt(jnp.mean(o * o, axis=-1, keepdims=True) + EPS)
        o_ref[0, rows, :] = (o * g_ref[0, rows, :].astype(F32)).astype(BF16)


def _ret_call(log_gamma, bq, bk, bv, gb):
    batch, seq, _ = bq.shape
    nc = seq // RET_CHUNK
    spec = pl.BlockSpec((1, seq, LANES), lambda b, h: (b, 0, h))
    return pl.pallas_call(
        functools.partial(_ret_kernel, seq=seq),
        grid=(batch, HEADS),
        in_specs=[pl.BlockSpec(memory_space=pltpu.SMEM), spec, spec, spec, spec],
        out_specs=spec,
        out_shape=jax.ShapeDtypeStruct((batch, seq, D_BRANCH), BF16),
        scratch_shapes=[pltpu.VMEM((nc, LANES, V_DIM), F32), pltpu.VMEM((nc, LANES, V_DIM), F32)],
        compiler_params=pltpu.CompilerParams(
            dimension_semantics=("arbitrary", "arbitrary"), vmem_limit_bytes=VMEM_LIMIT),
        name="retention",
    )(log_gamma, bq, bk, bv, gb)


def _out_kernel(oa_ref, ob_ref, cu_ref, cup_ref, cun_ref, gc_ref, x_ref, wo_ref, pw_ref, ps_ref,
                fnw_ref, o_ref, ext_scr, *, seq, final):
    tm = cu_ref.shape[0]
    tiles_per_seq = seq // tm
    it = pl.program_id(0) % tiles_per_seq

    ext_scr[0:POOL_HALO, :] = jnp.where(it == 0, 0.0, cup_ref[...])
    ext_scr[POOL_HALO:POOL_HALO + tm, :] = cu_ref[...]
    ext_scr[POOL_HALO + tm:, :] = jnp.where(it == tiles_per_seq - 1, 0.0, cun_ref[...])

    pos = it * tm + lax.broadcasted_iota(jnp.int32, (tm, 1), 0)
    ys = []
    for g, w in enumerate(POOL_WINDOWS):
        cols = slice(g * LANES, (g + 1) * LANES)
        wsum = None
        for d in range(-(w // 2), w - w // 2):
            sl = ext_scr[POOL_HALO + d:POOL_HALO + d + tm, cols]
            wsum = sl if wsum is None else wsum + sl
        lo = jnp.maximum(pos - w // 2, 0)
        hi = jnp.minimum(pos + (w - w // 2), seq)
        count = (hi - lo).astype(F32)
        pooled = wsum / count - cu_ref[:, cols]
        ys.append(jnp.dot(pooled.astype(BF16), pw_ref[g], preferred_element_type=F32))
    yc = jnp.concatenate(ys, axis=1) * ps_ref[...] * gc_ref[...].astype(F32)

    mixed = jnp.concatenate([oa_ref[...], ob_ref[...], yc.astype(BF16)], axis=1)
    out = x_ref[...] + jnp.dot(mixed, wo_ref[...], preferred_element_type=F32)
    if final:
        out = out * lax.rsqrt(jnp.mean(out * out, axis=-1, keepdims=True) + EPS) * fnw_ref[...]
    o_ref[...] = out


def _out_call(oa, ob, cu, gc, x2, w_out, pool_w, pool_scale, final_norm_w, seq, final):
    rows = x2.shape[0]
    tm = PROJ_ROWS
    halo_blocks = tm // POOL_HALO
    last_halo = rows // POOL_HALO - 1
    row_spec = lambda n: pl.BlockSpec((tm, n), lambda i: (i, 0))
    const = lambda shape: pl.BlockSpec(shape, lambda i: (0,) * len(shape))
    return pl.pallas_call(
        functools.partial(_out_kernel, seq=seq, final=final),
        grid=(rows // tm,),
        in_specs=[
            row_spec(D_BRANCH), row_spec(D_BRANCH), row_spec(D_BRANCH),
            pl.BlockSpec((POOL_HALO, D_BRANCH),
                         lambda i: (jnp.maximum(i * halo_blocks - 1, 0), 0)),
            pl.BlockSpec((POOL_HALO, D_BRANCH),
                         lambda i: (jnp.minimum((i + 1) * halo_blocks, last_halo), 0)),
            row_spec(D_BRANCH), row_spec(D_MODEL),
            const((3 * D_BRANCH, D_MODEL)),
            const((len(POOL_WINDOWS), LANES, LANES)),
            const((1, D_BRANCH)),
            const((1, D_MODEL)),
        ],
        out_specs=row_spec(D_MODEL),
        out_shape=jax.ShapeDtypeStruct((rows, D_MODEL), F32),
        scratch_shapes=[pltpu.VMEM((tm + 2 * POOL_HALO, D_BRANCH), F32)],
        compiler_params=pltpu.CompilerParams(
            dimension_semantics=("arbitrary",), vmem_limit_bytes=VMEM_LIMIT),
        name="pool_out",
    )(oa, ob, cu, cu, cu, gc, x2, w_out, pool_w, pool_scale, final_norm_w)


def _t5_bucket(rel):
    half = NUM_BUCKETS // 2
    max_exact = half // 2
    ret = jnp.where(rel > 0, half, 0)
    n = jnp.abs(rel)
    nf = jnp.maximum(n, 1).astype(F32)
    large = max_exact + (jnp.log(nf / max_exact) / math.log(MAX_DISTANCE / max_exact)
                         * (half - max_exact)).astype(jnp.int32)
    large = jnp.minimum(large, half - 1)
    return ret + jnp.where(n < max_exact, n, large)


def _bias_kernel(u_ref, o_ref):
    t = ATTN_Q
    for r in range(u_ref.shape[1]):
        x = jnp.broadcast_to(u_ref[0, r:r + 1, :], (t, 2 * t))
        o_ref[0, r] = pltpu.roll(x, t + 1, 1, stride=1, stride_axis=0)[:, :t]


def _bias_tiles(rel_bias, seq):
    t = ATTN_Q
    assert t >= MAX_DISTANCE and ATTN_K == t
    off = jnp.arange(2 * t, dtype=jnp.int32) - (t - 1)
    rel = jnp.stack([off - t, off, off + t, jnp.full_like(off, -seq), jnp.full_like(off, seq)])
    u = rel_bias.astype(F32)[_t5_bucket(rel)].transpose(2, 0, 1) * LOG2E
    n_tiles = rel.shape[0]
    return pl.pallas_call(
        _bias_kernel,
        grid=(HEADS,),
        in_specs=[pl.BlockSpec((1, n_tiles, 2 * t), lambda h: (h, 0, 0))],
        out_specs=pl.BlockSpec((1, n_tiles, t, t), lambda h: (h, 0, 0, 0)),
        out_shape=jax.ShapeDtypeStruct((HEADS, n_tiles, t, t), F32),
        compiler_params=pltpu.CompilerParams(dimension_semantics=("arbitrary",)),
        name="bias_tiles",
    )(u)


def _rotary_tables(seq):
    half = QK_DIM // 2
    theta = 1.0 / (ROPE_BASE ** jnp.linspace(0.0, 1.0, half, dtype=F32))
    ang = jnp.arange(seq, dtype=F32)[:, None] * theta[None, :]
    cos = jnp.cos(ang)
    sin = jnp.sin(ang)
    reps = LANES // QK_DIM
    cos_t = jnp.tile(jnp.concatenate([cos, cos], axis=1), (1, reps))
    sin_t = jnp.tile(jnp.concatenate([-sin, sin], axis=1), (1, reps))
    return cos_t, sin_t


def kernel(x, norm_w, w_in, diff_lambda, diff_subln_w, ret_decay_logit, pool_w, pool_scale, w_out,
           rel_bias, final_norm_w):
    batch, seq, d_model = x.shape
    assert d_model == D_MODEL and seq % PROJ_ROWS == 0 and seq % ATTN_Q == 0
    rows = batch * seq
    cos_t, sin_t = _rotary_tables(seq)
    bias_t = _bias_tiles(rel_bias, seq)
    fnw = final_norm_w.reshape(1, D_MODEL).astype(F32)

    h = x.reshape(rows, D_MODEL)
    for l in range(DEPTH):
        lam_init = 0.8 - 0.6 * math.exp(-0.3 * l)
        lf = diff_lambda[l].astype(F32)
        lam = (jnp.exp(jnp.sum(lf[0] * lf[1])) - jnp.exp(jnp.sum(lf[2] * lf[3])) + lam_init)
        log_gamma = jax.nn.log_sigmoid(ret_decay_logit[l].astype(F32))

        aq, ak, av, ga, bq, bk, bv, gb, cu, gc = _proj_call(
            h, norm_w[l].reshape(1, D_MODEL).astype(F32), w_in[l].astype(BF16), cos_t, sin_t, seq)
        to3 = lambda a: a.reshape(batch, seq, a.shape[-1])
        oa = _attn_call(lam.reshape(1), to3(aq), to3(ak), to3(av), to3(ga), bias_t,
                        diff_subln_w[l].reshape(1, V_DIM).astype(F32), lam_init)
        ob = _ret_call(log_gamma, to3(bq), to3(bk), to3(bv), to3(gb))
        h = _out_call(oa.reshape(rows, D_BRANCH), ob.reshape(rows, D_BRANCH), cu, gc, h,
                      w_out[l].astype(BF16), pool_w[l].astype(BF16),
                      pool_scale[l].reshape(1, D_BRANCH).astype(F32), fnw, seq,
                      final=(l == DEPTH - 1))
    return h.reshape(batch, seq, D_MODEL)
```

```python
import functools
import math

import jax
import jax.numpy as jnp
from jax import lax
from jax.experimental import pallas as pl
from jax.experimental.pallas import tpu as pltpu

F32 = jnp.float32
BF16 = jnp.bfloat16

D_MODEL = 1024
DEPTH = 2
D_BRANCH = 512
HEADS = 4
QK_DIM = 64
V_DIM = 128
RET_CHUNK = 128
ROPE_BASE = 10000.0
POOL_WINDOWS = (2, 4, 8, 16)
POOL_HALO = 8
NUM_BUCKETS = 32
MAX_DISTANCE = 128
EPS = 1e-6
LOG2E = math.log2(math.e)

_SIZES = (512, 512, 512, 512, 256, 256, 512, 512, 512, 512)
_OFFS = tuple(sum(_SIZES[:i]) for i in range(len(_SIZES)))
D_IN = sum(_SIZES)

LANES = 128
VMEM_LIMIT = 56 * 1024 * 1024

PROJ_ROWS = 512
ATTN_Q = 512
ATTN_K = 256
BIAS_T = ATTN_K


def _silu(x):
    return x * jax.nn.sigmoid(x)


def _proj_kernel(x_ref, nw_ref, w_ref, cos_ref, sin_ref,
                 aq_ref, ak_ref, av_ref, ga_ref, bq_ref, bk_ref, bv_ref, gb_ref, cu_ref, gc_ref):
    x = x_ref[...]
    ms = jnp.mean(x * x, axis=-1, keepdims=True)
    h = (x * lax.rsqrt(ms + EPS) * nw_ref[...]).astype(BF16)

    def proj(idx):
        lo = _OFFS[idx]
        return jnp.dot(h, w_ref[:, lo:lo + _SIZES[idx]], preferred_element_type=F32)

    aq_ref[...] = (proj(0) * (QK_DIM ** -0.5 * LOG2E)).astype(BF16)
    ak_ref[...] = proj(1).astype(BF16)
    av = proj(2).astype(BF16)
    ones = jnp.ones((av.shape[0], V_DIM), BF16)
    av_ref[...] = jnp.concatenate(
        [blk for hd in range(HEADS) for blk in (av[:, hd * V_DIM:(hd + 1) * V_DIM], ones)], axis=1)
    ga_ref[...] = _silu(proj(3)).astype(BF16)

    cos = cos_ref[...]
    sin = sin_ref[...]
    lane = lax.broadcasted_iota(jnp.int32, cos.shape, 1)
    first_half = (lane % QK_DIM) < (QK_DIM // 2)
    low_head = lane < QK_DIM

    def rotary_dup(t, scale):
        outs = []
        for half in range(2):
            th = t[:, half * LANES:(half + 1) * LANES]
            swapped = jnp.where(first_half, pltpu.roll(th, LANES - QK_DIM // 2, 1),
                                pltpu.roll(th, QK_DIM // 2, 1))
            r = (th * cos + swapped * sin) * scale
            rr = pltpu.roll(r, QK_DIM, 1)
            outs.append(jnp.where(low_head, r, rr))
            outs.append(jnp.where(low_head, rr, r))
        return jnp.concatenate(outs, axis=1).astype(BF16)

    bq_ref[...] = rotary_dup(proj(4), 1.0)
    bk_ref[...] = rotary_dup(proj(5), QK_DIM ** -0.5)
    bv_ref[...] = proj(6).astype(BF16)
    gb_ref[...] = _silu(proj(7)).astype(BF16)
    cu_ref[...] = proj(8)
    gc_ref[...] = _silu(proj(9)).astype(BF16)


def _proj_call(x2, norm_w, w_in, cos_t, sin_t, seq):
    rows = x2.shape[0]
    tm = PROJ_ROWS
    tiles_per_seq = seq // tm
    row_spec = lambda n: pl.BlockSpec((tm, n), lambda i: (i, 0))
    widths = [D_BRANCH] * 10
    widths[2] = 2 * D_BRANCH
    dtypes = [BF16] * 10
    dtypes[8] = F32
    out_shapes = [jax.ShapeDtypeStruct((rows, w), d) for w, d in zip(widths, dtypes)]
    return pl.pallas_call(
        _proj_kernel,
        grid=(rows // tm,),
        in_specs=[
            row_spec(D_MODEL),
            pl.BlockSpec((1, D_MODEL), lambda i: (0, 0)),
            pl.BlockSpec((D_MODEL, D_IN), lambda i: (0, 0)),
            pl.BlockSpec((tm, LANES), lambda i: (i % tiles_per_seq, 0)),
            pl.BlockSpec((tm, LANES), lambda i: (i % tiles_per_seq, 0)),
        ],
        out_specs=[row_spec(w) for w in widths],
        out_shape=out_shapes,
        compiler_params=pltpu.CompilerParams(
            dimension_semantics=("arbitrary",), vmem_limit_bytes=VMEM_LIMIT),
        name="proj",
    )(x2, norm_w, w_in, cos_t, sin_t)


def _attn_kernel(lam_ref, q_ref, k_ref, v_ref, g_ref, bias_ref, sw_ref, o_ref,
                 s_scr, m_scr, *, seq, nq, lam_init):
    tq = ATTN_Q
    j = pl.program_id(0)
    nkb = seq // ATTN_K

    @pl.when(j == 0)
    def _():
        s_scr[...] = jnp.zeros(s_scr.shape, F32)
        m_scr[...] = jnp.zeros(m_scr.shape, F32)

    qi = jnp.minimum(j, pl.num_programs(0) - 2) % nq
    q = q_ref[0]
    lane = lax.broadcasted_iota(jnp.int32, q.shape, 1)
    zero = jnp.zeros_like(q)
    q_st = jnp.concatenate([jnp.where(lane < QK_DIM, q, zero),
                            jnp.where(lane < QK_DIM, zero, q)], axis=0)

    m_b = m_scr[...]
    accs = [jnp.zeros((tq, 2 * V_DIM), F32) for _ in range(2)]
    m_acc = None
    for c in range(nkb):
        cols = slice(c * ATTN_K, (c + 1) * ATTN_K)
        sb = s_scr[:, cols]
        e = jnp.exp2(jnp.concatenate([sb[:, :LANES] - m_b, sb[:, LANES:] - m_b], axis=1))
        e = e.astype(BF16)
        vc = v_ref[0, cols, :]
        accs = [acc + jnp.dot(e[mp * tq:(mp + 1) * tq], vc, preferred_element_type=F32)
                for mp, acc in enumerate(accs)]
        s = lax.dot_general(q_st, k_ref[0, cols, :], (((1,), (1,)), ((), ())),
                            preferred_element_type=F32)
        tiles = []
        for a in range(tq // BIAS_T):
            rel = c - (qi * (tq // BIAS_T) + a)
            tiles.append(bias_ref[0, jnp.where(rel < -1, 3, jnp.where(rel > 1, 4, rel + 1))])
        s = s + jnp.concatenate(tiles + tiles, axis=0)
        s_scr[:, cols] = s
        bm = jnp.maximum(s[:, :LANES], s[:, LANES:])
        m_acc = bm if m_acc is None else jnp.maximum(m_acc, bm)
    m_new = jnp.max(m_acc, axis=1, keepdims=True)
    m_scr[...] = jnp.broadcast_to(m_new, m_acc.shape)

    on = [acc[:, :V_DIM] / acc[:, V_DIM:] for acc in accs]
    o = on[0] - lam_ref[0] * on[1]
    o = o * lax.rsqrt(jnp.mean(o * o, axis=-1, keepdims=True) + EPS) * sw_ref[...]
    o = o * (1.0 - lam_init)
    o_ref[0] = (o * g_ref[0].astype(F32)).astype(BF16)


def _attn_call(lam, aq, ak, av, ga, bias_t, subln_w, lam_init):
    batch, seq, _ = aq.shape
    tq = ATTN_Q
    nq = seq // tq
    n_tiles = batch * HEADS * nq
    kern = functools.partial(_attn_kernel, seq=seq, nq=nq, lam_init=lam_init)

    def front(j):
        t = jnp.minimum(j, n_tiles - 1)
        return t // (HEADS * nq), (t // nq) % HEADS, t % nq

    def back(j):
        t = jnp.maximum(j - 1, 0)
        return t // (HEADS * nq), (t // nq) % HEADS, t % nq

    def tile_of(where):
        return lambda j: (where(j)[0], where(j)[2], where(j)[1])

    def seq_of(where):
        return lambda j: (where(j)[0], 0, where(j)[1])

    return pl.pallas_call(
        kern,
        grid=(n_tiles + 1,),
        in_specs=[
            pl.BlockSpec(memory_space=pltpu.SMEM),
            pl.BlockSpec((1, tq, LANES), tile_of(front)),
            pl.BlockSpec((1, seq, LANES), seq_of(front)),
            pl.BlockSpec((1, seq, 2 * V_DIM), seq_of(back)),
            pl.BlockSpec((1, tq, LANES), tile_of(back)),
            pl.BlockSpec((1,) + bias_t.shape[1:], lambda j: (front(j)[1], 0, 0, 0)),
            pl.BlockSpec((1, V_DIM), lambda j: (0, 0)),
        ],
        out_specs=pl.BlockSpec((1, tq, LANES), tile_of(back)),
        out_shape=jax.ShapeDtypeStruct((batch, seq, D_BRANCH), BF16),
        scratch_shapes=[pltpu.VMEM((2 * tq, seq), F32), pltpu.VMEM((2 * tq, LANES), F32)],
        compiler_params=pltpu.CompilerParams(
            dimension_semantics=("arbitrary",), vmem_limit_bytes=VMEM_LIMIT),
        name="diff_attn",
    )(lam, aq, ak, av, ga, bias_t, subln_w)


def _ret_kernel(lg_ref, q_ref, k_ref, v_ref, g_ref, o_ref, kv_scr, r_scr, *, seq):
    C = RET_CHUNK
    nc = seq // C
    hd = pl.program_id(1)
    lgf = lg_ref[0, hd]
    lgb = lg_ref[1, hd]

    ri = lax.broadcasted_iota(jnp.int32, (C, LANES), 0).astype(F32)
    fwd_lane = lax.broadcasted_iota(jnp.int32, (C, LANES), 1) < QK_DIM
    q_scale = jnp.exp(jnp.where(fwd_lane, lgf * (ri + 1.0), lgb * (C - ri)))
    k_scale = jnp.exp(jnp.where(fwd_lane, lgf * (C - 1.0 - ri), lgb * ri))
    dn = (lax.broadcasted_iota(jnp.int32, (C, C), 0)
          - lax.broadcasted_iota(jnp.int32, (C, C), 1)).astype(F32)
    decay = 0.5 * jnp.where(dn >= 0.0, jnp.exp(lgf * jnp.maximum(dn, 0.0)),
                            jnp.exp(lgb * jnp.maximum(-dn, 0.0)))

    for c in range(nc):
        rows = slice(c * C, (c + 1) * C)
        kk = (k_ref[0, rows, :].astype(F32) * k_scale).astype(BF16)
        kv_scr[c] = lax.dot_general(kk, v_ref[0, rows, :], (((0,), (0,)), ((), ())),
                                    preferred_element_type=F32)

    top = lax.broadcasted_iota(jnp.int32, (LANES, V_DIM), 0) < QK_DIM
    dec_f = jnp.exp(lgf * C)
    dec_b = jnp.exp(lgb * C)
    state = jnp.zeros((LANES, V_DIM), F32)
    for c in range(nc):
        r_scr[c] = state
        state = state * dec_f + kv_scr[c]
    state = jnp.zeros((LANES, V_DIM), F32)
    for c in reversed(range(nc)):
        r_scr[c] = jnp.where(top, r_scr[c], state)
        state = state * dec_b + kv_scr[c]

    for c in range(nc):
        rows = slice(c * C, (c + 1) * C)
        qc = q_ref[0, rows, :]
        kc = k_ref[0, rows, :]
        s = lax.dot_general(qc, kc, (((1,), (1,)), ((), ())), preferred_element_type=F32)
        a = (s * decay).astype(BF16)
        qq = (qc.astype(F32) * q_scale).astype(BF16)
        o = (jnp.dot(a, v_ref[0, rows, :], preferred_element_type=F32)
             + jnp.dot(qq, r_scr[c].astype(BF16), preferred_element_type=F32))
        o = o * lax.rsqrt(jnp.mean(o * o, axis=-1, keepdims=True) + EPS)
        o_ref[0, rows, :] = (o * g_ref[0, rows, :].astype(F32)).astype(BF16)


def _ret_call(log_gamma, bq, bk, bv, gb):
    batch, seq, _ = bq.shape
    nc = seq // RET_CHUNK
    spec = pl.BlockSpec((1, seq, LANES), lambda b, h: (b, 0, h))
    return pl.pallas_call(
        functools.partial(_ret_kernel, seq=seq),
        grid=(batch, HEADS),
        in_specs=[pl.BlockSpec(memory_space=pltpu.SMEM), spec, spec, spec, spec],
        out_specs=spec,
        out_shape=jax.ShapeDtypeStruct((batch, seq, D_BRANCH), BF16),
        scratch_shapes=[pltpu.VMEM((nc, LANES, V_DIM), F32), pltpu.VMEM((nc, LANES, V_DIM), F32)],
        compiler_params=pltpu.CompilerParams(
            dimension_semantics=("arbitrary", "arbitrary"), vmem_limit_bytes=VMEM_LIMIT),
        name="retention",
    )(log_gamma, bq, bk, bv, gb)


def _out_kernel(oa_ref, ob_ref, cu_ref, cup_ref, cun_ref, gc_ref, x_ref, wo_ref, pw_ref, ps_ref,
                fnw_ref, o_ref, ext_scr, pooled_scr, *, seq, final):
    tm = cu_ref.shape[0]
    tiles_per_seq = seq // tm
    it = pl.program_id(0) % tiles_per_seq

    ext_scr[0:POOL_HALO, :] = jnp.where(it == 0, 0.0, cup_ref[...])
    ext_scr[POOL_HALO:POOL_HALO + tm, :] = cu_ref[...]
    ext_scr[POOL_HALO + tm:, :] = jnp.where(it == tiles_per_seq - 1, 0.0, cun_ref[...])

    n = tm + 2 * POOL_HALO
    body = slice(POOL_HALO, POOL_HALO + tm)
    pos = it * tm + lax.broadcasted_iota(jnp.int32, (tm, 1), 0)
    for g, w in enumerate(POOL_WINDOWS):
        cols = slice(g * LANES, (g + 1) * LANES)
        run = {1: ext_scr[:, cols]}
        k = 1
        while 2 * k <= min(w, POOL_HALO):
            run[2 * k] = run[k] + pltpu.roll(run[k], n - k, 0)
            k *= 2
        if w == 2 * POOL_HALO:
            wsum = run[POOL_HALO][0:tm] + run[POOL_HALO][body]
        else:
            wsum = pltpu.roll(run[w], w // 2, 0)[body]
        lo = jnp.maximum(pos - w // 2, 0)
        hi = jnp.minimum(pos + (w - w // 2), seq)
        count = (hi - lo).astype(F32)
        pooled = wsum / count - cu_ref[:, cols]
        pooled_scr[:, cols] = pooled.astype(BF16)

    out = x_ref[...] + jnp.dot(oa_ref[...], wo_ref[0:D_BRANCH, :], preferred_element_type=F32)
    out = out + jnp.dot(ob_ref[...], wo_ref[D_BRANCH:2 * D_BRANCH, :],
                        preferred_element_type=F32)
    ys = [jnp.dot(pooled_scr[:, g * LANES:(g + 1) * LANES], pw_ref[g],
                  preferred_element_type=F32) for g in range(len(POOL_WINDOWS))]
    yc = jnp.concatenate(ys, axis=1) * ps_ref[...] * gc_ref[...].astype(F32)
    out = out + jnp.dot(yc.astype(BF16), wo_ref[2 * D_BRANCH:, :], preferred_element_type=F32)
    if final:
        out = out * lax.rsqrt(jnp.mean(out * out, axis=-1, keepdims=True) + EPS) * fnw_ref[...]
    o_ref[...] = out


def _out_call(oa, ob, cu, gc, x2, w_out, pool_w, pool_scale, final_norm_w, seq, final):
    rows = x2.shape[0]
    tm = PROJ_ROWS
    halo_blocks = tm // POOL_HALO
    last_halo = rows // POOL_HALO - 1
    row_spec = lambda n: pl.BlockSpec((tm, n), lambda i: (i, 0))
    const = lambda shape: pl.BlockSpec(shape, lambda i: (0,) * len(shape))
    return pl.pallas_call(
        functools.partial(_out_kernel, seq=seq, final=final),
        grid=(rows // tm,),
        in_specs=[
            row_spec(D_BRANCH), row_spec(D_BRANCH), row_spec(D_BRANCH),
            pl.BlockSpec((POOL_HALO, D_BRANCH),
                         lambda i: (jnp.maximum(i * halo_blocks - 1, 0), 0)),
            pl.BlockSpec((POOL_HALO, D_BRANCH),
                         lambda i: (jnp.minimum((i + 1) * halo_blocks, last_halo), 0)),
            row_spec(D_BRANCH), row_spec(D_MODEL),
            const((3 * D_BRANCH, D_MODEL)),
            const((len(POOL_WINDOWS), LANES, LANES)),
            const((1, D_BRANCH)),
            const((1, D_MODEL)),
        ],
        out_specs=row_spec(D_MODEL),
        out_shape=jax.ShapeDtypeStruct((rows, D_MODEL), F32),
        scratch_shapes=[pltpu.VMEM((tm + 2 * POOL_HALO, D_BRANCH), F32),
                        pltpu.VMEM((tm, D_BRANCH), BF16)],
        compiler_params=pltpu.CompilerParams(
            dimension_semantics=("arbitrary",), vmem_limit_bytes=VMEM_LIMIT),
        name="pool_out",
    )(oa, ob, cu, cu, cu, gc, x2, w_out, pool_w, pool_scale, final_norm_w)


def _t5_bucket(rel):
    half = NUM_BUCKETS // 2
    max_exact = half // 2
    ret = jnp.where(rel > 0, half, 0)
    n = jnp.abs(rel)
    nf = jnp.maximum(n, 1).astype(F32)
    large = max_exact + (jnp.log(nf / max_exact) / math.log(MAX_DISTANCE / max_exact)
                         * (half - max_exact)).astype(jnp.int32)
    large = jnp.minimum(large, half - 1)
    return ret + jnp.where(n < max_exact, n, large)


def _bias_kernel(u_ref, o_ref):
    t = BIAS_T
    for r in range(u_ref.shape[1]):
        x = jnp.broadcast_to(u_ref[0, r:r + 1, :], (t, 2 * t))
        o_ref[0, r] = pltpu.roll(x, t + 1, 1, stride=1, stride_axis=0)[:, :t]


def _bias_tiles(rel_bias, seq):
    t = BIAS_T
    assert t >= MAX_DISTANCE and ATTN_Q % t == 0
    off = jnp.arange(2 * t, dtype=jnp.int32) - (t - 1)
    rel = jnp.stack([off - t, off, off + t, jnp.full_like(off, -seq), jnp.full_like(off, seq)])
    u = rel_bias.astype(F32)[_t5_bucket(rel)].transpose(2, 0, 1) * LOG2E
    n_tiles = rel.shape[0]
    return pl.pallas_call(
        _bias_kernel,
        grid=(HEADS,),
        in_specs=[pl.BlockSpec((1, n_tiles, 2 * t), lambda h: (h, 0, 0))],
        out_specs=pl.BlockSpec((1, n_tiles, t, t), lambda h: (h, 0, 0, 0)),
        out_shape=jax.ShapeDtypeStruct((HEADS, n_tiles, t, t), F32),
        compiler_params=pltpu.CompilerParams(dimension_semantics=("arbitrary",)),
        name="bias_tiles",
    )(u)


def _rotary_tables(seq):
    half = QK_DIM // 2
    theta = 1.0 / (ROPE_BASE ** jnp.linspace(0.0, 1.0, half, dtype=F32))
    ang = jnp.arange(seq, dtype=F32)[:, None] * theta[None, :]
    cos = jnp.cos(ang)
    sin = jnp.sin(ang)
    reps = LANES // QK_DIM
    cos_t = jnp.tile(jnp.concatenate([cos, cos], axis=1), (1, reps))
    sin_t = jnp.tile(jnp.concatenate([-sin, sin], axis=1), (1, reps))
    return cos_t, sin_t


def kernel(x, norm_w, w_in, diff_lambda, diff_subln_w, ret_decay_logit, pool_w, pool_scale, w_out,
           rel_bias, final_norm_w):
    batch, seq, d_model = x.shape
    assert d_model == D_MODEL and seq % PROJ_ROWS == 0 and seq % ATTN_Q == 0
    rows = batch * seq
    cos_t, sin_t = _rotary_tables(seq)
    bias_t = _bias_tiles(rel_bias, seq)
    fnw = final_norm_w.reshape(1, D_MODEL).astype(F32)

    h = x.reshape(rows, D_MODEL)
    for l in range(DEPTH):
        lam_init = 0.8 - 0.6 * math.exp(-0.3 * l)
        lf = diff_lambda[l].astype(F32)
        lam = (jnp.exp(jnp.sum(lf[0] * lf[1])) - jnp.exp(jnp.sum(lf[2] * lf[3])) + lam_init)
        log_gamma = jax.nn.log_sigmoid(ret_decay_logit[l].astype(F32))

        aq, ak, av, ga, bq, bk, bv, gb, cu, gc = _proj_call(
            h, norm_w[l].reshape(1, D_MODEL).astype(F32), w_in[l].astype(BF16), cos_t, sin_t, seq)
        to3 = lambda a: a.reshape(batch, seq, a.shape[-1])
        oa = _attn_call(lam.reshape(1), to3(aq), to3(ak), to3(av), to3(ga), bias_t,
                        diff_subln_w[l].reshape(1, V_DIM).astype(F32), lam_init)
        ob = _ret_call(log_gamma, to3(bq), to3(bk), to3(bv), to3(gb))
        h = _out_call(oa.reshape(rows, D_BRANCH), ob.reshape(rows, D_BRANCH), cu, gc, h,
                      w_out[l].astype(BF16), pool_w[l].astype(BF16),
                      pool_scale[l].reshape(1, D_BRANCH).astype(F32), fnw, seq,
                      final=(l == DEPTH - 1))
    return h.reshape(batch, seq, D_MODEL)
```

```python
import functools
import math

import jax
import jax.numpy as jnp
from jax import lax
from jax.experimental import pallas as pl
from jax.experimental.pallas import tpu as pltpu

F32 = jnp.float32
BF16 = jnp.bfloat16

D_MODEL = 1024
DEPTH = 2
D_BRANCH = 512
HEADS = 4
QK_DIM = 64
V_DIM = 128
RET_CHUNK = 128
ROPE_BASE = 10000.0
POOL_WINDOWS = (2, 4, 8, 16)
POOL_HALO = 8
NUM_BUCKETS = 32
MAX_DISTANCE = 128
EPS = 1e-6
LOG2E = math.log2(math.e)

_SIZES = (512, 512, 512, 512, 256, 256, 512, 512, 512, 512)
_OFFS = tuple(sum(_SIZES[:i]) for i in range(len(_SIZES)))
D_IN = sum(_SIZES)

LANES = 128
VMEM_LIMIT = 56 * 1024 * 1024

PROJ_ROWS = 512
OUT_ROWS = 1024
RET_HEADS = 4
ATTN_Q = 512
ATTN_SUB = 4
ATTN_K = 256
BIAS_T = 256


def _silu(x):
    return x * jax.nn.sigmoid(x)


def _proj_kernel(x_ref, nw_ref, w_ref, cos_ref, sin_ref,
                 aq_ref, ak_ref, av_ref, ga_ref, bq_ref, bk_ref, bv_ref, gb_ref, cu_ref, gc_ref):
    x = x_ref[...]
    ms = jnp.mean(x * x, axis=-1, keepdims=True)
    h = (x * lax.rsqrt(ms + EPS) * nw_ref[...]).astype(BF16)

    def proj(idx):
        lo = _OFFS[idx]
        return jnp.dot(h, w_ref[:, lo:lo + _SIZES[idx]], preferred_element_type=F32)

    aq_ref[...] = (proj(0) * (QK_DIM ** -0.5 * LOG2E)).astype(BF16)
    ak_ref[...] = proj(1).astype(BF16)
    av = proj(2).astype(BF16)
    ones = jnp.ones((av.shape[0], V_DIM), BF16)
    av_ref[...] = jnp.concatenate(
        [blk for hd in range(HEADS) for blk in (av[:, hd * V_DIM:(hd + 1) * V_DIM], ones)], axis=1)
    ga_ref[...] = _silu(proj(3)).astype(BF16)

    cos = cos_ref[...]
    sin = sin_ref[...]
    lane = lax.broadcasted_iota(jnp.int32, cos.shape, 1)
    first_half = (lane % QK_DIM) < (QK_DIM // 2)
    low_head = lane < QK_DIM

    def rotary_dup(t, scale):
        outs = []
        for half in range(2):
            th = t[:, half * LANES:(half + 1) * LANES]
            swapped = jnp.where(first_half, pltpu.roll(th, LANES - QK_DIM // 2, 1),
                                pltpu.roll(th, QK_DIM // 2, 1))
            r = (th * cos + swapped * sin) * scale
            rr = pltpu.roll(r, QK_DIM, 1)
            outs.append(jnp.where(low_head, r, rr))
            outs.append(jnp.where(low_head, rr, r))
        return jnp.concatenate(outs, axis=1).astype(BF16)

    bq_ref[...] = rotary_dup(proj(4), 1.0)
    bk_ref[...] = rotary_dup(proj(5), QK_DIM ** -0.5)
    bv_ref[...] = proj(6).astype(BF16)
    gb_ref[...] = _silu(proj(7)).astype(BF16)
    cu_ref[...] = proj(8)
    gc_ref[...] = _silu(proj(9)).astype(BF16)


def _proj_call(x2, norm_w, w_in, cos_t, sin_t, seq):
    rows = x2.shape[0]
    tm = PROJ_ROWS
    tiles_per_seq = seq // tm
    row_spec = lambda n: pl.BlockSpec((tm, n), lambda i: (i, 0))
    widths = [D_BRANCH] * 10
    widths[2] = 2 * D_BRANCH
    dtypes = [BF16] * 10
    dtypes[8] = F32
    out_shapes = [jax.ShapeDtypeStruct((rows, w), d) for w, d in zip(widths, dtypes)]
    return pl.pallas_call(
        _proj_kernel,
        grid=(rows // tm,),
        in_specs=[
            row_spec(D_MODEL),
            pl.BlockSpec((1, D_MODEL), lambda i: (0, 0)),
            pl.BlockSpec((D_MODEL, D_IN), lambda i: (0, 0)),
            pl.BlockSpec((tm, LANES), lambda i: (i % tiles_per_seq, 0)),
            pl.BlockSpec((tm, LANES), lambda i: (i % tiles_per_seq, 0)),
        ],
        out_specs=[row_spec(w) for w in widths],
        out_shape=out_shapes,
        compiler_params=pltpu.CompilerParams(
            dimension_semantics=("arbitrary",), vmem_limit_bytes=VMEM_LIMIT),
        name="proj",
    )(x2, norm_w, w_in, cos_t, sin_t)


def _attn_kernel(lam_ref, q_ref, k_ref, v_ref, g_ref, bias_ref, sw_ref, o_ref,
                 s_scr, m_scr, *, seq, nq, lam_init):
    tq = ATTN_Q
    j = pl.program_id(0)
    nkb = seq // ATTN_K

    @pl.when(j == 0)
    def _():
        s_scr[...] = jnp.zeros(s_scr.shape, F32)
        m_scr[...] = jnp.zeros(m_scr.shape, F32)

    step_in_seq = jnp.minimum(j, pl.num_programs(0) - 2) % nq
    lane = lax.broadcasted_iota(jnp.int32, (tq, LANES), 1)
    for sub in range(ATTN_SUB):
        rows = slice(sub * tq, (sub + 1) * tq)
        qi = step_in_seq * ATTN_SUB + sub
        q = q_ref[0, rows, :]
        zero = jnp.zeros_like(q)
        q_st = jnp.concatenate([jnp.where(lane < QK_DIM, q, zero),
                                jnp.where(lane < QK_DIM, zero, q)], axis=0)

        m_b = m_scr[sub]
        accs = [jnp.zeros((tq, 2 * V_DIM), F32) for _ in range(2)]
        m_acc = None
        for c in range(nkb):
            cols = slice(c * ATTN_K, (c + 1) * ATTN_K)
            sb = s_scr[sub, :, cols]
            e = jnp.exp2(jnp.concatenate(
                [sb[:, t * LANES:(t + 1) * LANES] - m_b for t in range(ATTN_K // LANES)],
                axis=1))
            e = e.astype(BF16)
            vc = v_ref[0, cols, :]
            accs = [acc + jnp.dot(e[mp * tq:(mp + 1) * tq], vc, preferred_element_type=F32)
                    for mp, acc in enumerate(accs)]
            s = lax.dot_general(q_st, k_ref[0, cols, :], (((1,), (1,)), ((), ())),
                                preferred_element_type=F32)
            tiles = []
            for a in range(tq // BIAS_T):
                row = []
                for kb in range(ATTN_K // BIAS_T):
                    rel = (c * (ATTN_K // BIAS_T) + kb) - (qi * (tq // BIAS_T) + a)
                    row.append(
                        bias_ref[0, jnp.where(rel < -1, 3, jnp.where(rel > 1, 4, rel + 1))])
                tiles.append(jnp.concatenate(row, axis=1))
            s = s + jnp.concatenate(tiles + tiles, axis=0)
            s_scr[sub, :, cols] = s
            bm = s[:, :LANES]
            for t in range(1, ATTN_K // LANES):
                bm = jnp.maximum(bm, s[:, t * LANES:(t + 1) * LANES])
            m_acc = bm if m_acc is None else jnp.maximum(m_acc, bm)
        m_new = jnp.max(m_acc, axis=1, keepdims=True)
        m_scr[sub] = jnp.broadcast_to(m_new, m_acc.shape)

        on = [acc[:, :V_DIM] / acc[:, V_DIM:] for acc in accs]
        o = on[0] - lam_ref[0] * on[1]
        o = o * lax.rsqrt(jnp.mean(o * o, axis=-1, keepdims=True) + EPS) * sw_ref[...]
        o = o * (1.0 - lam_init)
        o_ref[0, rows, :] = (o * g_ref[0, rows, :].astype(F32)).astype(BF16)


def _attn_call(lam, aq, ak, av, ga, bias_t, subln_w, lam_init):
    batch, seq, _ = aq.shape
    tq = ATTN_Q * ATTN_SUB
    nq = seq // tq
    n_tiles = batch * HEADS * nq
    kern = functools.partial(_attn_kernel, seq=seq, nq=nq, lam_init=lam_init)

    def front(j):
        t = jnp.minimum(j, n_tiles - 1)
        return t // (HEADS * nq), (t // nq) % HEADS, t % nq

    def back(j):
        t = jnp.maximum(j - 1, 0)
        return t // (HEADS * nq), (t // nq) % HEADS, t % nq

    def tile_of(where):
        return lambda j: (where(j)[0], where(j)[2], where(j)[1])

    def seq_of(where):
        return lambda j: (where(j)[0], 0, where(j)[1])

    return pl.pallas_call(
        kern,
        grid=(n_tiles + 1,),
        in_specs=[
            pl.BlockSpec(memory_space=pltpu.SMEM),
            pl.BlockSpec((1, tq, LANES), tile_of(front)),
            pl.BlockSpec((1, seq, LANES), seq_of(front)),
            pl.BlockSpec((1, seq, 2 * V_DIM), seq_of(back)),
            pl.BlockSpec((1, tq, LANES), tile_of(back)),
            pl.BlockSpec((1,) + bias_t.shape[1:], lambda j: (front(j)[1], 0, 0, 0)),
            pl.BlockSpec((1, V_DIM), lambda j: (0, 0)),
        ],
        out_specs=pl.BlockSpec((1, tq, LANES), tile_of(back)),
        out_shape=jax.ShapeDtypeStruct((batch, seq, D_BRANCH), BF16),
        scratch_shapes=[pltpu.VMEM((ATTN_SUB, 2 * ATTN_Q, seq), F32),
                        pltpu.VMEM((ATTN_SUB, 2 * ATTN_Q, LANES), F32)],
        compiler_params=pltpu.CompilerParams(
            dimension_semantics=("arbitrary",), vmem_limit_bytes=VMEM_LIMIT),
        name="diff_attn",
    )(lam, aq, ak, av, ga, bias_t, subln_w)


def _ret_kernel(lg_ref, q_ref, k_ref, v_ref, g_ref, o_ref, kv_scr, r_scr, *, seq):
    C = RET_CHUNK
    nc = seq // C
    ri = lax.broadcasted_iota(jnp.int32, (C, LANES), 0).astype(F32)
    fwd_lane = lax.broadcasted_iota(jnp.int32, (C, LANES), 1) < QK_DIM
    dn = (lax.broadcasted_iota(jnp.int32, (C, C), 0)
          - lax.broadcasted_iota(jnp.int32, (C, C), 1)).astype(F32)
    top = lax.broadcasted_iota(jnp.int32, (LANES, V_DIM), 0) < QK_DIM

    for hh in range(RET_HEADS):
        hd = pl.program_id(1) * RET_HEADS + hh
        hl = slice(hh * LANES, (hh + 1) * LANES)
        lgf = lg_ref[0, hd]
        lgb = lg_ref[1, hd]
        q_scale = jnp.exp(jnp.where(fwd_lane, lgf * (ri + 1.0), lgb * (C - ri)))
        k_scale = jnp.exp(jnp.where(fwd_lane, lgf * (C - 1.0 - ri), lgb * ri))
        decay = 0.5 * jnp.where(dn >= 0.0, jnp.exp(lgf * jnp.maximum(dn, 0.0)),
                                jnp.exp(lgb * jnp.maximum(-dn, 0.0)))

        for c in range(nc):
            rows = slice(c * C, (c + 1) * C)
            kk = (k_ref[0, rows, hl].astype(F32) * k_scale).astype(BF16)
            kv_scr[hh * nc + c] = lax.dot_general(kk, v_ref[0, rows, hl], (((0,), (0,)), ((), ())),
                                                  preferred_element_type=F32)

        dec_f = jnp.exp(lgf * C)
        dec_b = jnp.exp(lgb * C)
        state = jnp.zeros((LANES, V_DIM), F32)
        for c in range(nc):
            r_scr[hh * nc + c] = state
            state = state * dec_f + kv_scr[hh * nc + c]
        state = jnp.zeros((LANES, V_DIM), F32)
        for c in reversed(range(nc)):
            r_scr[hh * nc + c] = jnp.where(top, r_scr[hh * nc + c], state)
            state = state * dec_b + kv_scr[hh * nc + c]

        for c in range(nc):
            rows = slice(c * C, (c + 1) * C)
            qc = q_ref[0, rows, hl]
            kc = k_ref[0, rows, hl]
            s = lax.dot_general(qc, kc, (((1,), (1,)), ((), ())), preferred_element_type=F32)
            a = (s * decay).astype(BF16)
            qq = (qc.astype(F32) * q_scale).astype(BF16)
            o = (jnp.dot(a, v_ref[0, rows, hl], preferred_element_type=F32)
                 + jnp.dot(qq, r_scr[hh * nc + c].astype(BF16), preferred_element_type=F32))
            o = o * lax.rsqrt(jnp.mean(o * o, axis=-1, keepdims=True) + EPS)
            o_ref[0, rows, hl] = (o * g_ref[0, rows, hl].astype(F32)).astype(BF16)


def _ret_call(log_gamma, bq, bk, bv, gb):
    batch, seq, _ = bq.shape
    nc = seq // RET_CHUNK
    spec = pl.BlockSpec((1, seq, RET_HEADS * LANES), lambda b, h: (b, 0, h))
    state_shape = (RET_HEADS * nc, LANES, V_DIM)
    return pl.pallas_call(
        functools.partial(_ret_kernel, seq=seq),
        grid=(batch, HEADS // RET_HEADS),
        in_specs=[pl.BlockSpec(memory_space=pltpu.SMEM), spec, spec, spec, spec],
        out_specs=spec,
        out_shape=jax.ShapeDtypeStruct((batch, seq, D_BRANCH), BF16),
        scratch_shapes=[pltpu.VMEM(state_shape, F32), pltpu.VMEM(state_shape, F32)],
        compiler_params=pltpu.CompilerParams(
            dimension_semantics=("arbitrary", "arbitrary"), vmem_limit_bytes=VMEM_LIMIT),
        name="retention",
    )(log_gamma, bq, bk, bv, gb)


def _out_kernel(oa_ref, ob_ref, cu_ref, cup_ref, cun_ref, gc_ref, x_ref, wo_ref, pw_ref, ps_ref,
                fnw_ref, o_ref, ext_scr, pooled_scr, *, seq, final):
    tm = cu_ref.shape[0]
    tiles_per_seq = seq // tm
    it = pl.program_id(0) % tiles_per_seq

    ext_scr[0:POOL_HALO, :] = jnp.where(it == 0, 0.0, cup_ref[...])
    ext_scr[POOL_HALO:POOL_HALO + tm, :] = cu_ref[...]
    ext_scr[POOL_HALO + tm:, :] = jnp.where(it == tiles_per_seq - 1, 0.0, cun_ref[...])

    n = tm + 2 * POOL_HALO
    body = slice(POOL_HALO, POOL_HALO + tm)
    pos = it * tm + lax.broadcasted_iota(jnp.int32, (tm, 1), 0)
    for g, w in enumerate(POOL_WINDOWS):
        cols = slice(g * LANES, (g + 1) * LANES)
        run = {1: ext_scr[:, cols]}
        k = 1
        while 2 * k <= min(w, POOL_HALO):
            run[2 * k] = run[k] + pltpu.roll(run[k], n - k, 0)
            k *= 2
        if w == 2 * POOL_HALO:
            wsum = run[POOL_HALO][0:tm] + run[POOL_HALO][body]
        else:
            wsum = pltpu.roll(run[w], w // 2, 0)[body]
        lo = jnp.maximum(pos - w // 2, 0)
        hi = jnp.minimum(pos + (w - w // 2), seq)
        count = (hi - lo).astype(F32)
        pooled = wsum / count - cu_ref[:, cols]
        pooled_scr[:, cols] = pooled.astype(BF16)

    out = x_ref[...] + jnp.dot(oa_ref[...], wo_ref[0:D_BRANCH, :], preferred_element_type=F32)
    out = out + jnp.dot(ob_ref[...], wo_ref[D_BRANCH:2 * D_BRANCH, :],
                        preferred_element_type=F32)
    ys = [jnp.dot(pooled_scr[:, g * LANES:(g + 1) * LANES], pw_ref[g],
                  preferred_element_type=F32) for g in range(len(POOL_WINDOWS))]
    yc = jnp.concatenate(ys, axis=1) * ps_ref[...] * gc_ref[...].astype(F32)
    out = out + jnp.dot(yc.astype(BF16), wo_ref[2 * D_BRANCH:, :], preferred_element_type=F32)
    if final:
        out = out * lax.rsqrt(jnp.mean(out * out, axis=-1, keepdims=True) + EPS) * fnw_ref[...]
    o_ref[...] = out


def _out_call(oa, ob, cu, gc, x2, w_out, pool_w, pool_scale, final_norm_w, seq, final):
    rows = x2.shape[0]
    tm = OUT_ROWS
    halo_blocks = tm // POOL_HALO
    last_halo = rows // POOL_HALO - 1
    row_spec = lambda n: pl.BlockSpec((tm, n), lambda i: (i, 0))
    const = lambda shape: pl.BlockSpec(shape, lambda i: (0,) * len(shape))
    return pl.pallas_call(
        functools.partial(_out_kernel, seq=seq, final=final),
        grid=(rows // tm,),
        in_specs=[
            row_spec(D_BRANCH), row_spec(D_BRANCH), row_spec(D_BRANCH),
            pl.BlockSpec((POOL_HALO, D_BRANCH),
                         lambda i: (jnp.maximum(i * halo_blocks - 1, 0), 0)),
            pl.BlockSpec((POOL_HALO, D_BRANCH),
                         lambda i: (jnp.minimum((i + 1) * halo_blocks, last_halo), 0)),
            row_spec(D_BRANCH), row_spec(D_MODEL),
            const((3 * D_BRANCH, D_MODEL)),
            const((len(POOL_WINDOWS), LANES, LANES)),
            const((1, D_BRANCH)),
            const((1, D_MODEL)),
        ],
        out_specs=row_spec(D_MODEL),
        out_shape=jax.ShapeDtypeStruct((rows, D_MODEL), F32),
        scratch_shapes=[pltpu.VMEM((tm + 2 * POOL_HALO, D_BRANCH), F32),
                        pltpu.VMEM((tm, D_BRANCH), BF16)],
        compiler_params=pltpu.CompilerParams(
            dimension_semantics=("arbitrary",), vmem_limit_bytes=VMEM_LIMIT),
        name="pool_out",
    )(oa, ob, cu, cu, cu, gc, x2, w_out, pool_w, pool_scale, final_norm_w)


def _t5_bucket(rel):
    half = NUM_BUCKETS // 2
    max_exact = half // 2
    ret = jnp.where(rel > 0, half, 0)
    n = jnp.abs(rel)
    nf = jnp.maximum(n, 1).astype(F32)
    large = max_exact + (jnp.log(nf / max_exact) / math.log(MAX_DISTANCE / max_exact)
                         * (half - max_exact)).astype(jnp.int32)
    large = jnp.minimum(large, half - 1)
    return ret + jnp.where(n < max_exact, n, large)


def _bias_kernel(u_ref, o_ref):
    t = BIAS_T
    for r in range(u_ref.shape[1]):
        x = jnp.broadcast_to(u_ref[0, r:r + 1, :], (t, 2 * t))
        o_ref[0, r] = pltpu.roll(x, t + 1, 1, stride=1, stride_axis=0)[:, :t]


def _bias_tiles(rel_bias, seq):
    t = BIAS_T
    assert t >= MAX_DISTANCE and ATTN_Q % t == 0
    off = jnp.arange(2 * t, dtype=jnp.int32) - (t - 1)
    rel = jnp.stack([off - t, off, off + t, jnp.full_like(off, -seq), jnp.full_like(off, seq)])
    u = rel_bias.astype(F32)[_t5_bucket(rel)].transpose(2, 0, 1) * LOG2E
    n_tiles = rel.shape[0]
    return pl.pallas_call(
        _bias_kernel,
        grid=(HEADS,),
        in_specs=[pl.BlockSpec((1, n_tiles, 2 * t), lambda h: (h, 0, 0))],
        out_specs=pl.BlockSpec((1, n_tiles, t, t), lambda h: (h, 0, 0, 0)),
        out_shape=jax.ShapeDtypeStruct((HEADS, n_tiles, t, t), F32),
        compiler_params=pltpu.CompilerParams(dimension_semantics=("arbitrary",)),
        name="bias_tiles",
    )(u)


def _rotary_tables(seq):
    half = QK_DIM // 2
    theta = 1.0 / (ROPE_BASE ** jnp.linspace(0.0, 1.0, half, dtype=F32))
    ang = jnp.arange(seq, dtype=F32)[:, None] * theta[None, :]
    cos = jnp.cos(ang)
    sin = jnp.sin(ang)
    reps = LANES // QK_DIM
    cos_t = jnp.tile(jnp.concatenate([cos, cos], axis=1), (1, reps))
    sin_t = jnp.tile(jnp.concatenate([-sin, sin], axis=1), (1, reps))
    return cos_t, sin_t


def kernel(x, norm_w, w_in, diff_lambda, diff_subln_w, ret_decay_logit, pool_w, pool_scale, w_out,
           rel_bias, final_norm_w):
    batch, seq, d_model = x.shape
    assert d_model == D_MODEL and seq % PROJ_ROWS == 0 and seq % OUT_ROWS == 0
    assert seq % (ATTN_Q * ATTN_SUB) == 0
    rows = batch * seq
    cos_t, sin_t = _rotary_tables(seq)
    bias_t = _bias_tiles(rel_bias, seq)
    fnw = final_norm_w.reshape(1, D_MODEL).astype(F32)

    h = x.reshape(rows, D_MODEL)
    for l in range(DEPTH):
        lam_init = 0.8 - 0.6 * math.exp(-0.3 * l)
        lf = diff_lambda[l].astype(F32)
        lam = (jnp.exp(jnp.sum(lf[0] * lf[1])) - jnp.exp(jnp.sum(lf[2] * lf[3])) + lam_init)
        log_gamma = jax.nn.log_sigmoid(ret_decay_logit[l].astype(F32))

        aq, ak, av, ga, bq, bk, bv, gb, cu, gc = _proj_call(
            h, norm_w[l].reshape(1, D_MODEL).astype(F32), w_in[l].astype(BF16), cos_t, sin_t, seq)
        to3 = lambda a: a.reshape(batch, seq, a.shape[-1])
        oa = _attn_call(lam.reshape(1), to3(aq), to3(ak), to3(av), to3(ga), bias_t,
                        diff_subln_w[l].reshape(1, V_DIM).astype(F32), lam_init)
        ob = _ret_call(log_gamma, to3(bq), to3(bk), to3(bv), to3(gb))
        h = _out_call(oa.reshape(rows, D_BRANCH), ob.reshape(rows, D_BRANCH), cu, gc, h,
                      w_out[l].astype(BF16), pool_w[l].astype(BF16),
                      pool_scale[l].reshape(1, D_BRANCH).astype(F32), fnw, seq,
                      final=(l == DEPTH - 1))
    return h.reshape(batch, seq, D_MODEL)
```

```python
import functools
import math

import jax
import jax.numpy as jnp
from jax import lax
from jax.experimental import pallas as pl
from jax.experimental.pallas import tpu as pltpu

F32 = jnp.float32
BF16 = jnp.bfloat16

D_MODEL = 1024
DEPTH = 2
D_BRANCH = 512
HEADS = 4
QK_DIM = 64
V_DIM = 128
RET_CHUNK = 128
ROPE_BASE = 10000.0
POOL_WINDOWS = (2, 4, 8, 16)
POOL_HALO = 8
EDGE_ROWS = 16
NUM_BUCKETS = 32
MAX_DISTANCE = 128
EPS = 1e-6
LOG2E = math.log2(math.e)

_SIZES = (512, 512, 512, 512, 256, 256, 512, 512, 512, 512)
_OFFS = tuple(sum(_SIZES[:i]) for i in range(len(_SIZES)))
D_IN = sum(_SIZES)

LANES = 128
VMEM_LIMIT = 56 * 1024 * 1024

PROJ_ROWS = 1024
OUT_ROWS = 1024
RET_HEADS = 4
ATTN_Q = 512
ATTN_SUB = 4
ATTN_K = 256
BIAS_T = 256


def _silu(x):
    return x * jax.nn.sigmoid(x)


def _proj_kernel(x_ref, nw_ref, w_ref, cos_ref, sin_ref,
                 aq_ref, ak_ref, av_ref, ga_ref, bq_ref, bk_ref, bv_ref, gb_ref, cu_ref, gc_ref):
    x = x_ref[...]
    ms = jnp.mean(x * x, axis=-1, keepdims=True)
    h = (x * lax.rsqrt(ms + EPS) * nw_ref[...]).astype(BF16)

    def proj(idx):
        lo = _OFFS[idx]
        return jnp.dot(h, w_ref[:, lo:lo + _SIZES[idx]], preferred_element_type=F32)

    aq_ref[...] = (proj(0) * (QK_DIM ** -0.5 * LOG2E)).astype(BF16)
    ak_ref[...] = proj(1).astype(BF16)
    av = proj(2).astype(BF16)
    ones = jnp.ones((av.shape[0], V_DIM), BF16)
    av_ref[...] = jnp.concatenate(
        [blk for hd in range(HEADS) for blk in (av[:, hd * V_DIM:(hd + 1) * V_DIM], ones)], axis=1)
    ga_ref[...] = _silu(proj(3)).astype(BF16)

    cos = cos_ref[...]
    sin = sin_ref[...]
    lane = lax.broadcasted_iota(jnp.int32, cos.shape, 1)
    first_half = (lane % QK_DIM) < (QK_DIM // 2)
    low_head = lane < QK_DIM

    def rotary_dup(t, scale):
        outs = []
        for half in range(2):
            th = t[:, half * LANES:(half + 1) * LANES]
            swapped = jnp.where(first_half, pltpu.roll(th, LANES - QK_DIM // 2, 1),
                                pltpu.roll(th, QK_DIM // 2, 1))
            r = (th * cos + swapped * sin) * scale
            rr = pltpu.roll(r, QK_DIM, 1)
            outs.append(jnp.where(low_head, r, rr))
            outs.append(jnp.where(low_head, rr, r))
        return jnp.concatenate(outs, axis=1).astype(BF16)

    bq_ref[...] = rotary_dup(proj(4), 1.0)
    bk_ref[...] = rotary_dup(proj(5), QK_DIM ** -0.5)
    bv_ref[...] = proj(6).astype(BF16)
    gb_ref[...] = _silu(proj(7)).astype(BF16)
    cu_ref[...] = proj(8)
    gc_ref[...] = _silu(proj(9)).astype(BF16)


def _proj_call(x2, norm_w, w_in, cos_t, sin_t, seq):
    rows = x2.shape[0]
    tm = PROJ_ROWS
    tiles_per_seq = seq // tm
    row_spec = lambda n: pl.BlockSpec((tm, n), lambda i: (i, 0))
    widths = [D_BRANCH] * 10
    widths[2] = 2 * D_BRANCH
    dtypes = [BF16] * 10
    dtypes[8] = F32
    out_shapes = [jax.ShapeDtypeStruct((rows, w), d) for w, d in zip(widths, dtypes)]
    return pl.pallas_call(
        _proj_kernel,
        grid=(rows // tm,),
        in_specs=[
            row_spec(D_MODEL),
            pl.BlockSpec((1, D_MODEL), lambda i: (0, 0)),
            pl.BlockSpec((D_MODEL, D_IN), lambda i: (0, 0), pipeline_mode=pl.Buffered(1)),
            pl.BlockSpec((tm, LANES), lambda i: (i % tiles_per_seq, 0)),
            pl.BlockSpec((tm, LANES), lambda i: (i % tiles_per_seq, 0)),
        ],
        out_specs=[row_spec(w) for w in widths],
        out_shape=out_shapes,
        compiler_params=pltpu.CompilerParams(
            dimension_semantics=("arbitrary",), vmem_limit_bytes=VMEM_LIMIT),
        name="proj",
    )(x2, norm_w, w_in, cos_t, sin_t)


def _attn_kernel(lam_ref, q_ref, k_ref, v_ref, g_ref, bias_ref, sw_ref, o_ref,
                 s_scr, m_scr, *, seq, nq, lam_init):
    tq = ATTN_Q
    j = pl.program_id(0)
    nkb = seq // ATTN_K

    @pl.when(j == 0)
    def _():
        s_scr[...] = jnp.zeros(s_scr.shape, F32)
        m_scr[...] = jnp.zeros(m_scr.shape, F32)

    step_in_seq = jnp.minimum(j, pl.num_programs(0) - 2) % nq
    lane = lax.broadcasted_iota(jnp.int32, (tq, LANES), 1)
    for sub in range(ATTN_SUB):
        rows = slice(sub * tq, (sub + 1) * tq)
        qi = step_in_seq * ATTN_SUB + sub
        q = q_ref[0, rows, :]
        zero = jnp.zeros_like(q)
        q_st = jnp.concatenate([jnp.where(lane < QK_DIM, q, zero),
                                jnp.where(lane < QK_DIM, zero, q)], axis=0)

        m_b = m_scr[sub]
        accs = [jnp.zeros((tq, 2 * V_DIM), F32) for _ in range(2)]
        m_acc = None
        for c in range(nkb):
            cols = slice(c * ATTN_K, (c + 1) * ATTN_K)
            sb = s_scr[sub, :, cols]
            e = jnp.exp2(jnp.concatenate(
                [sb[:, t * LANES:(t + 1) * LANES] - m_b for t in range(ATTN_K // LANES)],
                axis=1))
            e = e.astype(BF16)
            vc = v_ref[0, cols, :]
            accs = [acc + jnp.dot(e[mp * tq:(mp + 1) * tq], vc, preferred_element_type=F32)
                    for mp, acc in enumerate(accs)]
            s = lax.dot_general(q_st, k_ref[0, cols, :], (((1,), (1,)), ((), ())),
                                preferred_element_type=F32)
            tiles = []
            for a in range(tq // BIAS_T):
                row = []
                for kb in range(ATTN_K // BIAS_T):
                    rel = (c * (ATTN_K // BIAS_T) + kb) - (qi * (tq // BIAS_T) + a)
                    row.append(
                        bias_ref[0, jnp.where(rel < -1, 3, jnp.where(rel > 1, 4, rel + 1))])
                tiles.append(jnp.concatenate(row, axis=1))
            s = s + jnp.concatenate(tiles + tiles, axis=0)
            s_scr[sub, :, cols] = s
            bm = s[:, :LANES]
            for t in range(1, ATTN_K // LANES):
                bm = jnp.maximum(bm, s[:, t * LANES:(t + 1) * LANES])
            m_acc = bm if m_acc is None else jnp.maximum(m_acc, bm)
        m_new = jnp.max(m_acc, axis=1, keepdims=True)
        m_scr[sub] = jnp.broadcast_to(m_new, m_acc.shape)

        on = [acc[:, :V_DIM] / acc[:, V_DIM:] for acc in accs]
        o = on[0] - lam_ref[0] * on[1]
        o = o * lax.rsqrt(jnp.mean(o * o, axis=-1, keepdims=True) + EPS) * sw_ref[...]
        o = o * (1.0 - lam_init)
        o_ref[0, rows, :] = (o * g_ref[0, rows, :].astype(F32)).astype(BF16)


def _attn_call(lam, aq, ak, av, ga, bias_t, subln_w, lam_init):
    batch, seq, _ = aq.shape
    tq = ATTN_Q * ATTN_SUB
    nq = seq // tq
    n_tiles = batch * HEADS * nq
    kern = functools.partial(_attn_kernel, seq=seq, nq=nq, lam_init=lam_init)

    def front(j):
        t = jnp.minimum(j, n_tiles - 1)
        return t // (HEADS * nq), (t // nq) % HEADS, t % nq

    def back(j):
        t = jnp.maximum(j - 1, 0)
        return t // (HEADS * nq), (t // nq) % HEADS, t % nq

    def tile_of(where):
        return lambda j: (where(j)[0], where(j)[2], where(j)[1])

    def seq_of(where):
        return lambda j: (where(j)[0], 0, where(j)[1])

    return pl.pallas_call(
        kern,
        grid=(n_tiles + 1,),
        in_specs=[
            pl.BlockSpec(memory_space=pltpu.SMEM),
            pl.BlockSpec((1, tq, LANES), tile_of(front)),
            pl.BlockSpec((1, seq, LANES), seq_of(front)),
            pl.BlockSpec((1, seq, 2 * V_DIM), seq_of(back)),
            pl.BlockSpec((1, tq, LANES), tile_of(back)),
            pl.BlockSpec((1,) + bias_t.shape[1:], lambda j: (front(j)[1], 0, 0, 0)),
            pl.BlockSpec((1, V_DIM), lambda j: (0, 0)),
        ],
        out_specs=pl.BlockSpec((1, tq, LANES), tile_of(back)),
        out_shape=jax.ShapeDtypeStruct((batch, seq, D_BRANCH), BF16),
        scratch_shapes=[pltpu.VMEM((ATTN_SUB, 2 * ATTN_Q, seq), F32),
                        pltpu.VMEM((ATTN_SUB, 2 * ATTN_Q, LANES), F32)],
        compiler_params=pltpu.CompilerParams(
            dimension_semantics=("arbitrary",), vmem_limit_bytes=VMEM_LIMIT),
        name="diff_attn",
    )(lam, aq, ak, av, ga, bias_t, subln_w)


def _ret_kernel(lg_ref, q_ref, k_ref, v_ref, g_ref, o_ref, kv_scr, r_scr, *, seq):
    C = RET_CHUNK
    nc = seq // C
    ri = lax.broadcasted_iota(jnp.int32, (C, LANES), 0).astype(F32)
    fwd_lane = lax.broadcasted_iota(jnp.int32, (C, LANES), 1) < QK_DIM
    dn = (lax.broadcasted_iota(jnp.int32, (C, C), 0)
          - lax.broadcasted_iota(jnp.int32, (C, C), 1)).astype(F32)
    top = lax.broadcasted_iota(jnp.int32, (LANES, V_DIM), 0) < QK_DIM

    for hh in range(RET_HEADS):
        hd = pl.program_id(1) * RET_HEADS + hh
        hl = slice(hh * LANES, (hh + 1) * LANES)
        lgf = lg_ref[0, hd]
        lgb = lg_ref[1, hd]
        q_scale = jnp.exp(jnp.where(fwd_lane, lgf * (ri + 1.0), lgb * (C - ri)))
        k_scale = jnp.exp(jnp.where(fwd_lane, lgf * (C - 1.0 - ri), lgb * ri))
        decay = 0.5 * jnp.where(dn >= 0.0, jnp.exp(lgf * jnp.maximum(dn, 0.0)),
                                jnp.exp(lgb * jnp.maximum(-dn, 0.0)))

        for c in range(nc):
            rows = slice(c * C, (c + 1) * C)
            kk = (k_ref[0, rows, hl].astype(F32) * k_scale).astype(BF16)
            kv_scr[hh * nc + c] = lax.dot_general(kk, v_ref[0, rows, hl], (((0,), (0,)), ((), ())),
                                                  preferred_element_type=F32)

        dec_f = jnp.exp(lgf * C)
        dec_b = jnp.exp(lgb * C)
        state = jnp.zeros((LANES, V_DIM), F32)
        for c in range(nc):
            r_scr[hh * nc + c] = state
            state = state * dec_f + kv_scr[hh * nc + c]
        state = jnp.zeros((LANES, V_DIM), F32)
        for c in reversed(range(nc)):
            r_scr[hh * nc + c] = jnp.where(top, r_scr[hh * nc + c], state)
            state = state * dec_b + kv_scr[hh * nc + c]

        for c in range(nc):
            rows = slice(c * C, (c + 1) * C)
            qc = q_ref[0, rows, hl]
            kc = k_ref[0, rows, hl]
            s = lax.dot_general(qc, kc, (((1,), (1,)), ((), ())), preferred_element_type=F32)
            a = (s * decay).astype(BF16)
            qq = (qc.astype(F32) * q_scale).astype(BF16)
            o = (jnp.dot(a, v_ref[0, rows, hl], preferred_element_type=F32)
                 + jnp.dot(qq, r_scr[hh * nc + c].astype(BF16), preferred_element_type=F32))
            o = o * lax.rsqrt(jnp.mean(o * o, axis=-1, keepdims=True) + EPS)
            o_ref[0, rows, hl] = (o * g_ref[0, rows, hl].astype(F32)).astype(BF16)


def _ret_call(log_gamma, bq, bk, bv, gb):
    batch, seq, _ = bq.shape
    nc = seq // RET_CHUNK
    spec = pl.BlockSpec((1, seq, RET_HEADS * LANES), lambda b, h: (b, 0, h))
    state_shape = (RET_HEADS * nc, LANES, V_DIM)
    return pl.pallas_call(
        functools.partial(_ret_kernel, seq=seq),
        grid=(batch, HEADS // RET_HEADS),
        in_specs=[pl.BlockSpec(memory_space=pltpu.SMEM), spec, spec, spec, spec],
        out_specs=spec,
        out_shape=jax.ShapeDtypeStruct((batch, seq, D_BRANCH), BF16),
        scratch_shapes=[pltpu.VMEM(state_shape, F32), pltpu.VMEM(state_shape, F32)],
        compiler_params=pltpu.CompilerParams(
            dimension_semantics=("arbitrary", "arbitrary"), vmem_limit_bytes=VMEM_LIMIT),
        name="retention",
    )(log_gamma, bq, bk, bv, gb)


def _out_kernel(oa_ref, ob_ref, cu_ref, cup_ref, cun_ref, gc_ref, x_ref, wo_ref, pw_ref, ps_ref,
                fnw_ref, o_ref, ext_scr, pooled_scr, *, seq, final):
    tm = cu_ref.shape[0]
    tiles_per_seq = seq // tm
    it = pl.program_id(0) % tiles_per_seq

    ext_scr[0:POOL_HALO, :] = jnp.where(it == 0, 0.0, cup_ref[...])
    ext_scr[POOL_HALO:POOL_HALO + tm, :] = cu_ref[...]
    ext_scr[POOL_HALO + tm:, :] = jnp.where(it == tiles_per_seq - 1, 0.0, cun_ref[...])

    n = tm + 2 * POOL_HALO
    body = slice(POOL_HALO, POOL_HALO + tm)
    for g, w in enumerate(POOL_WINDOWS):
        cols = slice(g * LANES, (g + 1) * LANES)
        run = {1: ext_scr[:, cols]}
        k = 1
        while 2 * k <= min(w, POOL_HALO):
            run[2 * k] = run[k] + pltpu.roll(run[k], n - k, 0)
            k *= 2
        if w == 2 * POOL_HALO:
            wsum = run[POOL_HALO][0:tm] + run[POOL_HALO][body]
        else:
            wsum = pltpu.roll(run[w], w // 2, 0)[body]
        u = cu_ref[:, cols]
        pooled_scr[:, cols] = (wsum * (1.0 / w) - u).astype(BF16)
        for r0 in (0, tm - EDGE_ROWS):
            edge = slice(r0, r0 + EDGE_ROWS)
            pos = it * tm + r0 + lax.broadcasted_iota(jnp.int32, (EDGE_ROWS, 1), 0)
            lo = jnp.maximum(pos - w // 2, 0)
            hi = jnp.minimum(pos + (w - w // 2), seq)
            count = (hi - lo).astype(F32)
            pooled_scr[edge, cols] = (wsum[edge] / count - u[edge]).astype(BF16)

    out = x_ref[...] + jnp.dot(oa_ref[...], wo_ref[0:D_BRANCH, :], preferred_element_type=F32)
    out = out + jnp.dot(ob_ref[...], wo_ref[D_BRANCH:2 * D_BRANCH, :],
                        preferred_element_type=F32)
    ys = [jnp.dot(pooled_scr[:, g * LANES:(g + 1) * LANES], pw_ref[g],
                  preferred_element_type=F32) for g in range(len(POOL_WINDOWS))]
    yc = jnp.concatenate(ys, axis=1) * ps_ref[...] * gc_ref[...].astype(F32)
    out = out + jnp.dot(yc.astype(BF16), wo_ref[2 * D_BRANCH:, :], preferred_element_type=F32)
    if final:
        out = out * lax.rsqrt(jnp.mean(out * out, axis=-1, keepdims=True) + EPS) * fnw_ref[...]
    o_ref[...] = out


def _out_call(oa, ob, cu, gc, x2, w_out, pool_w, pool_scale, final_norm_w, seq, final):
    rows = x2.shape[0]
    tm = OUT_ROWS
    halo_blocks = tm // POOL_HALO
    last_halo = rows // POOL_HALO - 1
    row_spec = lambda n: pl.BlockSpec((tm, n), lambda i: (i, 0))
    const = lambda shape: pl.BlockSpec(shape, lambda i: (0,) * len(shape))
    return pl.pallas_call(
        functools.partial(_out_kernel, seq=seq, final=final),
        grid=(rows // tm,),
        in_specs=[
            row_spec(D_BRANCH), row_spec(D_BRANCH), row_spec(D_BRANCH),
            pl.BlockSpec((POOL_HALO, D_BRANCH),
                         lambda i: (jnp.maximum(i * halo_blocks - 1, 0), 0)),
            pl.BlockSpec((POOL_HALO, D_BRANCH),
                         lambda i: (jnp.minimum((i + 1) * halo_blocks, last_halo), 0)),
            row_spec(D_BRANCH), row_spec(D_MODEL),
            const((3 * D_BRANCH, D_MODEL)),
            const((len(POOL_WINDOWS), LANES, LANES)),
            const((1, D_BRANCH)),
            const((1, D_MODEL)),
        ],
        out_specs=row_spec(D_MODEL),
        out_shape=jax.ShapeDtypeStruct((rows, D_MODEL), F32),
        scratch_shapes=[pltpu.VMEM((tm + 2 * POOL_HALO, D_BRANCH), F32),
                        pltpu.VMEM((tm, D_BRANCH), BF16)],
        compiler_params=pltpu.CompilerParams(
            dimension_semantics=("arbitrary",), vmem_limit_bytes=VMEM_LIMIT),
        name="pool_out",
    )(oa, ob, cu, cu, cu, gc, x2, w_out, pool_w, pool_scale, final_norm_w)


def _t5_bucket(rel):
    half = NUM_BUCKETS // 2
    max_exact = half // 2
    ret = jnp.where(rel > 0, half, 0)
    n = jnp.abs(rel)
    nf = jnp.maximum(n, 1).astype(F32)
    large = max_exact + (jnp.log(nf / max_exact) / math.log(MAX_DISTANCE / max_exact)
                         * (half - max_exact)).astype(jnp.int32)
    large = jnp.minimum(large, half - 1)
    return ret + jnp.where(n < max_exact, n, large)


def _bias_kernel(u_ref, o_ref):
    t = BIAS_T
    for r in range(u_ref.shape[1]):
        x = jnp.broadcast_to(u_ref[0, r:r + 1, :], (t, 2 * t))
        o_ref[0, r] = pltpu.roll(x, t + 1, 1, stride=1, stride_axis=0)[:, :t]


def _bias_tiles(rel_bias, seq):
    t = BIAS_T
    assert t >= MAX_DISTANCE and ATTN_Q % t == 0
    off = jnp.arange(2 * t, dtype=jnp.int32) - (t - 1)
    rel = jnp.stack([off - t, off, off + t, jnp.full_like(off, -seq), jnp.full_like(off, seq)])
    u = rel_bias.astype(F32)[_t5_bucket(rel)].transpose(2, 0, 1) * LOG2E
    n_tiles = rel.shape[0]
    return pl.pallas_call(
        _bias_kernel,
        grid=(HEADS,),
        in_specs=[pl.BlockSpec((1, n_tiles, 2 * t), lambda h: (h, 0, 0))],
        out_specs=pl.BlockSpec((1, n_tiles, t, t), lambda h: (h, 0, 0, 0)),
        out_shape=jax.ShapeDtypeStruct((HEADS, n_tiles, t, t), F32),
        compiler_params=pltpu.CompilerParams(dimension_semantics=("arbitrary",)),
        name="bias_tiles",
    )(u)


def _rotary_tables(seq):
    half = QK_DIM // 2
    theta = 1.0 / (ROPE_BASE ** jnp.linspace(0.0, 1.0, half, dtype=F32))
    ang = jnp.arange(seq, dtype=F32)[:, None] * theta[None, :]
    cos = jnp.cos(ang)
    sin = jnp.sin(ang)
    reps = LANES // QK_DIM
    cos_t = jnp.tile(jnp.concatenate([cos, cos], axis=1), (1, reps))
    sin_t = jnp.tile(jnp.concatenate([-sin, sin], axis=1), (1, reps))
    return cos_t, sin_t


def kernel(x, norm_w, w_in, diff_lambda, diff_subln_w, ret_decay_logit, pool_w, pool_scale, w_out,
           rel_bias, final_norm_w):
    batch, seq, d_model = x.shape
    assert d_model == D_MODEL and seq % PROJ_ROWS == 0 and seq % OUT_ROWS == 0
    assert seq % (ATTN_Q * ATTN_SUB) == 0
    rows = batch * seq
    cos_t, sin_t = _rotary_tables(seq)
    bias_t = _bias_tiles(rel_bias, seq)
    fnw = final_norm_w.reshape(1, D_MODEL).astype(F32)

    h = x.reshape(rows, D_MODEL)
    for l in range(DEPTH):
        lam_init = 0.8 - 0.6 * math.exp(-0.3 * l)
        lf = diff_lambda[l].astype(F32)
        lam = (jnp.exp(jnp.sum(lf[0] * lf[1])) - jnp.exp(jnp.sum(lf[2] * lf[3])) + lam_init)
        log_gamma = jax.nn.log_sigmoid(ret_decay_logit[l].astype(F32))

        aq, ak, av, ga, bq, bk, bv, gb, cu, gc = _proj_call(
            h, norm_w[l].reshape(1, D_MODEL).astype(F32), w_in[l].astype(BF16), cos_t, sin_t, seq)
        to3 = lambda a: a.reshape(batch, seq, a.shape[-1])
        oa = _attn_call(lam.reshape(1), to3(aq), to3(ak), to3(av), to3(ga), bias_t,
                        diff_subln_w[l].reshape(1, V_DIM).astype(F32), lam_init)
        ob = _ret_call(log_gamma, to3(bq), to3(bk), to3(bv), to3(gb))
        h = _out_call(oa.reshape(rows, D_BRANCH), ob.reshape(rows, D_BRANCH), cu, gc, h,
                      w_out[l].astype(BF16), pool_w[l].astype(BF16),
                      pool_scale[l].reshape(1, D_BRANCH).astype(F32), fnw, seq,
                      final=(l == DEPTH - 1))
    return h.reshape(batch, seq, D_MODEL)
```

```python
import functools
import math

import jax
import jax.numpy as jnp
from jax import lax
from jax.experimental import pallas as pl
from jax.experimental.pallas import tpu as pltpu

F32 = jnp.float32
BF16 = jnp.bfloat16

D_MODEL = 1024
DEPTH = 2
D_BRANCH = 512
HEADS = 4
QK_DIM = 64
V_DIM = 128
RET_CHUNK = 128
ROPE_BASE = 10000.0
POOL_WINDOWS = (2, 4, 8, 16)
POOL_HALO = 8
EDGE_ROWS = 16
NUM_BUCKETS = 32
MAX_DISTANCE = 128
EPS = 1e-6
LOG2E = math.log2(math.e)

_SIZES = (512, 512, 512, 512, 256, 256, 512, 512, 512, 512)
_OFFS = tuple(sum(_SIZES[:i]) for i in range(len(_SIZES)))
D_IN = sum(_SIZES)

LANES = 128
VMEM_LIMIT = 56 * 1024 * 1024

PROJ_ROWS = 1024
OUT_ROWS = 1024
RET_HEADS = 4
ATTN_Q = 512
ATTN_K = 256
BIAS_T = 256


def _silu(x):
    return x * jax.nn.sigmoid(x)


def _proj_kernel(x_ref, nw_ref, w_ref, cos_ref, sin_ref,
                 aq_ref, ak_ref, av_ref, ga_ref, bq_ref, bk_ref, bv_ref, gb_ref, cu_ref, gc_ref):
    x = x_ref[...]
    ms = jnp.mean(x * x, axis=-1, keepdims=True)
    h = (x * lax.rsqrt(ms + EPS) * nw_ref[...]).astype(BF16)

    def proj(idx):
        lo = _OFFS[idx]
        return jnp.dot(h, w_ref[:, lo:lo + _SIZES[idx]], preferred_element_type=F32)

    aq_ref[...] = (proj(0) * (QK_DIM ** -0.5 * LOG2E)).astype(BF16)
    ak_ref[...] = proj(1).astype(BF16)
    av_ref[0] = proj(2).T.astype(BF16)
    ga_ref[...] = _silu(proj(3)).astype(BF16)

    cos = cos_ref[...]
    sin = sin_ref[...]
    lane = lax.broadcasted_iota(jnp.int32, cos.shape, 1)
    first_half = (lane % QK_DIM) < (QK_DIM // 2)
    low_head = lane < QK_DIM

    def rotary_dup(t, scale):
        outs = []
        for half in range(2):
            th = t[:, half * LANES:(half + 1) * LANES]
            swapped = jnp.where(first_half, pltpu.roll(th, LANES - QK_DIM // 2, 1),
                                pltpu.roll(th, QK_DIM // 2, 1))
            r = (th * cos + swapped * sin) * scale
            rr = pltpu.roll(r, QK_DIM, 1)
            outs.append(jnp.where(low_head, r, rr))
            outs.append(jnp.where(low_head, rr, r))
        return jnp.concatenate(outs, axis=1).astype(BF16)

    bq_ref[...] = rotary_dup(proj(4), 1.0)
    bk_ref[...] = rotary_dup(proj(5), QK_DIM ** -0.5)
    bv_ref[...] = proj(6).astype(BF16)
    gb_ref[...] = _silu(proj(7)).astype(BF16)
    cu_ref[...] = proj(8)
    gc_ref[...] = _silu(proj(9)).astype(BF16)


def _proj_call(x2, norm_w, w_in, cos_t, sin_t, seq):
    rows = x2.shape[0]
    tm = PROJ_ROWS
    tiles_per_seq = seq // tm
    row_spec = lambda n: pl.BlockSpec((tm, n), lambda i: (i, 0))
    dtypes = [BF16] * 10
    dtypes[8] = F32
    out_shapes = [jax.ShapeDtypeStruct((rows, D_BRANCH), d) for d in dtypes]
    out_specs = [row_spec(D_BRANCH)] * 10
    out_shapes[2] = jax.ShapeDtypeStruct((rows // seq, D_BRANCH, seq), BF16)
    out_specs[2] = pl.BlockSpec((1, D_BRANCH, tm), lambda i: (i // tiles_per_seq, 0,
                                                              i % tiles_per_seq))
    return pl.pallas_call(
        _proj_kernel,
        grid=(rows // tm,),
        in_specs=[
            row_spec(D_MODEL),
            pl.BlockSpec((1, D_MODEL), lambda i: (0, 0)),
            pl.BlockSpec((D_MODEL, D_IN), lambda i: (0, 0), pipeline_mode=pl.Buffered(1)),
            pl.BlockSpec((tm, LANES), lambda i: (i % tiles_per_seq, 0)),
            pl.BlockSpec((tm, LANES), lambda i: (i % tiles_per_seq, 0)),
        ],
        out_specs=out_specs,
        out_shape=out_shapes,
        compiler_params=pltpu.CompilerParams(
            dimension_semantics=("arbitrary",), vmem_limit_bytes=VMEM_LIMIT),
        name="proj",
    )(x2, norm_w, w_in, cos_t, sin_t)


def _attn_kernel(lam_ref, q_ref, k_ref, vt_ref, g_ref, bias_ref, sw_ref, o_ref,
                 s_scr, m_scr, e_scr, c_scr, a_scr, *, seq, nq, lam_init):
    tq = ATTN_Q
    nb = ATTN_K
    j = pl.program_id(0)
    nkb = seq // BIAS_T

    @pl.when(j == 0)
    def _():
        s_scr[...] = jnp.zeros(s_scr.shape, F32)
        m_scr[...] = jnp.zeros(m_scr.shape, F32)
        e_scr[...] = jnp.zeros(e_scr.shape, BF16)
        c_scr[...] = jnp.zeros(c_scr.shape, BF16)
        a_scr[...] = jnp.zeros(a_scr.shape, F32)

    qi = jnp.minimum(j, pl.num_programs(0) - 3) % nq

    c_b = c_scr[0:1, :]
    attn_t = e_scr[:, 0:tq] - c_b * e_scr[:, tq:]
    o_t = jnp.dot(vt_ref[0], attn_t, preferred_element_type=F32)
    o = (o_t * a_scr[0:1, :]).T
    o = o * lax.rsqrt(jnp.mean(o * o, axis=-1, keepdims=True) + EPS) * sw_ref[...]
    o = o * (1.0 - lam_init)
    o_ref[0] = (o * g_ref[0].astype(F32)).astype(BF16)

    q = q_ref[0]
    lane = lax.broadcasted_iota(jnp.int32, q.shape, 1)
    zero = jnp.zeros_like(q)
    q_st = jnp.concatenate([jnp.where(lane < QK_DIM, q, zero),
                            jnp.where(lane < QK_DIM, zero, q)], axis=0)
    k_all = k_ref[0]
    m_old = m_scr[0:1, :]
    m_new = []
    l_new = []
    for n in range(2 * tq // nb):
        cols = slice(n * nb, (n + 1) * nb)
        e = jnp.exp2(s_scr[:, cols] - m_old[:, cols])
        e_scr[:, cols] = e.astype(BF16)
        l_new.append(jnp.sum(e, axis=0, keepdims=True))
        s = lax.dot_general(k_all, q_st[cols, :], (((1,), (1,)), ((), ())),
                            preferred_element_type=F32)
        qb = qi * (tq // BIAS_T) + (n % (tq // nb)) * (nb // BIAS_T)
        tiles = []
        for c in range(nkb):
            rel = c - qb
            tiles.append(bias_ref[0, jnp.where(rel < -1, 3, jnp.where(rel > 1, 4, rel + 1))])
        s = s + jnp.concatenate(tiles, axis=0)
        s_scr[:, cols] = s
        m_new.append(jnp.max(s, axis=0, keepdims=True))
    m_scr[...] = jnp.broadcast_to(jnp.concatenate(m_new, axis=1), m_scr.shape)

    l_row = jnp.concatenate(l_new, axis=1)
    l1, l2 = l_row[:, :tq], l_row[:, tq:]
    c_scr[...] = jnp.broadcast_to(lam_ref[0] * l1 / l2, c_scr.shape).astype(BF16)
    a_scr[...] = jnp.broadcast_to(1.0 / l1, a_scr.shape)


def _attn_call(lam, aq, ak, avt, ga, bias_t, subln_w, lam_init):
    batch, seq, _ = aq.shape
    tq = ATTN_Q
    nq = seq // tq
    n_tiles = batch * HEADS * nq
    kern = functools.partial(_attn_kernel, seq=seq, nq=nq, lam_init=lam_init)

    def front(j):
        t = jnp.minimum(j, n_tiles - 1)
        return t // (HEADS * nq), (t // nq) % HEADS, t % nq

    def back(j):
        t = jnp.maximum(j - 2, 0)
        return t // (HEADS * nq), (t // nq) % HEADS, t % nq

    def tile_of(where):
        return lambda j: (where(j)[0], where(j)[2], where(j)[1])

    return pl.pallas_call(
        kern,
        grid=(n_tiles + 2,),
        in_specs=[
            pl.BlockSpec(memory_space=pltpu.SMEM),
            pl.BlockSpec((1, tq, LANES), tile_of(front)),
            pl.BlockSpec((1, seq, LANES), lambda j: (front(j)[0], 0, front(j)[1])),
            pl.BlockSpec((1, V_DIM, seq), lambda j: (back(j)[0], back(j)[1], 0)),
            pl.BlockSpec((1, tq, LANES), tile_of(back)),
            pl.BlockSpec((1,) + bias_t.shape[1:], lambda j: (front(j)[1], 0, 0, 0)),
            pl.BlockSpec((1, V_DIM), lambda j: (0, 0)),
        ],
        out_specs=pl.BlockSpec((1, tq, LANES), tile_of(back)),
        out_shape=jax.ShapeDtypeStruct((batch, seq, D_BRANCH), BF16),
        scratch_shapes=[
            pltpu.VMEM((seq, 2 * tq), F32),
            pltpu.VMEM((8, 2 * tq), F32),
            pltpu.VMEM((seq, 2 * tq), BF16),
            pltpu.VMEM((16, tq), BF16),
            pltpu.VMEM((8, tq), F32),
        ],
        compiler_params=pltpu.CompilerParams(
            dimension_semantics=("arbitrary",), vmem_limit_bytes=VMEM_LIMIT),
        name="diff_attn",
    )(lam, aq, ak, avt, ga, bias_t, subln_w)


def _ret_kernel(lg_ref, q_ref, k_ref, v_ref, g_ref, o_ref, kv_scr, r_scr, *, seq):
    C = RET_CHUNK
    nc = seq // C
    ri = lax.broadcasted_iota(jnp.int32, (C, LANES), 0).astype(F32)
    fwd_lane = lax.broadcasted_iota(jnp.int32, (C, LANES), 1) < QK_DIM
    dn = (lax.broadcasted_iota(jnp.int32, (C, C), 0)
          - lax.broadcasted_iota(jnp.int32, (C, C), 1)).astype(F32)
    top = lax.broadcasted_iota(jnp.int32, (LANES, V_DIM), 0) < QK_DIM

    for hh in range(RET_HEADS):
        hd = pl.program_id(1) * RET_HEADS + hh
        hl = slice(hh * LANES, (hh + 1) * LANES)
        lgf = lg_ref[0, hd]
        lgb = lg_ref[1, hd]
        q_scale = jnp.exp(jnp.where(fwd_lane, lgf * (ri + 1.0), lgb * (C - ri)))
        k_scale = jnp.exp(jnp.where(fwd_lane, lgf * (C - 1.0 - ri), lgb * ri))
        decay = 0.5 * jnp.where(dn >= 0.0, jnp.exp(lgf * jnp.maximum(dn, 0.0)),
                                jnp.exp(lgb * jnp.maximum(-dn, 0.0)))

        for c in range(nc):
            rows = slice(c * C, (c + 1) * C)
            kk = (k_ref[0, rows, hl].astype(F32) * k_scale).astype(BF16)
            kv_scr[hh * nc + c] = lax.dot_general(kk, v_ref[0, rows, hl], (((0,), (0,)), ((), ())),
                                                  preferred_element_type=F32)

        dec_f = jnp.exp(lgf * C)
        dec_b = jnp.exp(lgb * C)
        state = jnp.zeros((LANES, V_DIM), F32)
        for c in range(nc):
            r_scr[hh * nc + c] = state
            state = state * dec_f + kv_scr[hh * nc + c]
        state = jnp.zeros((LANES, V_DIM), F32)
        for c in reversed(range(nc)):
            r_scr[hh * nc + c] = jnp.where(top, r_scr[hh * nc + c], state)
            state = state * dec_b + kv_scr[hh * nc + c]

        for c in range(nc):
            rows = slice(c * C, (c + 1) * C)
            qc = q_ref[0, rows, hl]
            kc = k_ref[0, rows, hl]
            s = lax.dot_general(qc, kc, (((1,), (1,)), ((), ())), preferred_element_type=F32)
            a = (s * decay).astype(BF16)
            qq = (qc.astype(F32) * q_scale).astype(BF16)
            o = (jnp.dot(a, v_ref[0, rows, hl], preferred_element_type=F32)
                 + jnp.dot(qq, r_scr[hh * nc + c].astype(BF16), preferred_element_type=F32))
            o = o * lax.rsqrt(jnp.mean(o * o, axis=-1, keepdims=True) + EPS)
            o_ref[0, rows, hl] = (o * g_ref[0, rows, hl].astype(F32)).astype(BF16)


def _ret_call(log_gamma, bq, bk, bv, gb):
    batch, seq, _ = bq.shape
    nc = seq // RET_CHUNK
    spec = pl.BlockSpec((1, seq, RET_HEADS * LANES), lambda b, h: (b, 0, h))
    state_shape = (RET_HEADS * nc, LANES, V_DIM)
    return pl.pallas_call(
        functools.partial(_ret_kernel, seq=seq),
        grid=(batch, HEADS // RET_HEADS),
        in_specs=[pl.BlockSpec(memory_space=pltpu.SMEM), spec, spec, spec, spec],
        out_specs=spec,
        out_shape=jax.ShapeDtypeStruct((batch, seq, D_BRANCH), BF16),
        scratch_shapes=[pltpu.VMEM(state_shape, F32), pltpu.VMEM(state_shape, F32)],
        compiler_params=pltpu.CompilerParams(
            dimension_semantics=("arbitrary", "arbitrary"), vmem_limit_bytes=VMEM_LIMIT),
        name="retention",
    )(log_gamma, bq, bk, bv, gb)


def _out_kernel(oa_ref, ob_ref, cu_ref, cup_ref, cun_ref, gc_ref, x_ref, wo_ref, pw_ref, ps_ref,
                fnw_ref, o_ref, ext_scr, pooled_scr, *, seq, final):
    tm = cu_ref.shape[0]
    tiles_per_seq = seq // tm
    it = pl.program_id(0) % tiles_per_seq

    ext_scr[0:POOL_HALO, :] = jnp.where(it == 0, 0.0, cup_ref[...])
    ext_scr[POOL_HALO:POOL_HALO + tm, :] = cu_ref[...]
    ext_scr[POOL_HALO + tm:, :] = jnp.where(it == tiles_per_seq - 1, 0.0, cun_ref[...])

    n = tm + 2 * POOL_HALO
    body = slice(POOL_HALO, POOL_HALO + tm)
    for g, w in enumerate(POOL_WINDOWS):
        cols = slice(g * LANES, (g + 1) * LANES)
        run = {1: ext_scr[:, cols]}
        k = 1
        while 2 * k <= min(w, POOL_HALO):
            run[2 * k] = run[k] + pltpu.roll(run[k], n - k, 0)
            k *= 2
        if w == 2 * POOL_HALO:
            wsum = run[POOL_HALO][0:tm] + run[POOL_HALO][body]
        else:
            wsum = pltpu.roll(run[w], w // 2, 0)[body]
        u = cu_ref[:, cols]
        pooled_scr[:, cols] = (wsum * (1.0 / w) - u).astype(BF16)
        for r0 in (0, tm - EDGE_ROWS):
            edge = slice(r0, r0 + EDGE_ROWS)
            pos = it * tm + r0 + lax.broadcasted_iota(jnp.int32, (EDGE_ROWS, 1), 0)
            lo = jnp.maximum(pos - w // 2, 0)
            hi = jnp.minimum(pos + (w - w // 2), seq)
            count = (hi - lo).astype(F32)
            pooled_scr[edge, cols] = (wsum[edge] / count - u[edge]).astype(BF16)

    out = x_ref[...] + jnp.dot(oa_ref[...], wo_ref[0:D_BRANCH, :], preferred_element_type=F32)
    out = out + jnp.dot(ob_ref[...], wo_ref[D_BRANCH:2 * D_BRANCH, :],
                        preferred_element_type=F32)
    ys = [jnp.dot(pooled_scr[:, g * LANES:(g + 1) * LANES], pw_ref[g],
                  preferred_element_type=F32) for g in range(len(POOL_WINDOWS))]
    yc = jnp.concatenate(ys, axis=1) * ps_ref[...] * gc_ref[...].astype(F32)
    out = out + jnp.dot(yc.astype(BF16), wo_ref[2 * D_BRANCH:, :], preferred_element_type=F32)
    if final:
        out = out * lax.rsqrt(jnp.mean(out * out, axis=-1, keepdims=True) + EPS) * fnw_ref[...]
    o_ref[...] = out


def _out_call(oa, ob, cu, gc, x2, w_out, pool_w, pool_scale, final_norm_w, seq, final):
    rows = x2.shape[0]
    tm = OUT_ROWS
    halo_blocks = tm // POOL_HALO
    last_halo = rows // POOL_HALO - 1
    row_spec = lambda n: pl.BlockSpec((tm, n), lambda i: (i, 0))
    const = lambda shape: pl.BlockSpec(shape, lambda i: (0,) * len(shape))
    return pl.pallas_call(
        functools.partial(_out_kernel, seq=seq, final=final),
        grid=(rows // tm,),
        in_specs=[
            row_spec(D_BRANCH), row_spec(D_BRANCH), row_spec(D_BRANCH),
            pl.BlockSpec((POOL_HALO, D_BRANCH),
                         lambda i: (jnp.maximum(i * halo_blocks - 1, 0), 0)),
            pl.BlockSpec((POOL_HALO, D_BRANCH),
                         lambda i: (jnp.minimum((i + 1) * halo_blocks, last_halo), 0)),
            row_spec(D_BRANCH), row_spec(D_MODEL),
            const((3 * D_BRANCH, D_MODEL)),
            const((len(POOL_WINDOWS), LANES, LANES)),
            const((1, D_BRANCH)),
            const((1, D_MODEL)),
        ],
        out_specs=row_spec(D_MODEL),
        out_shape=jax.ShapeDtypeStruct((rows, D_MODEL), F32),
        scratch_shapes=[pltpu.VMEM((tm + 2 * POOL_HALO, D_BRANCH), F32),
                        pltpu.VMEM((tm, D_BRANCH), BF16)],
        compiler_params=pltpu.CompilerParams(
            dimension_semantics=("arbitrary",), vmem_limit_bytes=VMEM_LIMIT),
        name="pool_out",
    )(oa, ob, cu, cu, cu, gc, x2, w_out, pool_w, pool_scale, final_norm_w)


def _t5_bucket(rel):
    half = NUM_BUCKETS // 2
    max_exact = half // 2
    ret = jnp.where(rel > 0, half, 0)
    n = jnp.abs(rel)
    nf = jnp.maximum(n, 1).astype(F32)
    large = max_exact + (jnp.log(nf / max_exact) / math.log(MAX_DISTANCE / max_exact)
                         * (half - max_exact)).astype(jnp.int32)
    large = jnp.minimum(large, half - 1)
    return ret + jnp.where(n < max_exact, n, large)


def _bias_kernel(u_ref, o_ref):
    t = BIAS_T
    for r in range(u_ref.shape[1]):
        x = jnp.broadcast_to(u_ref[0, r:r + 1, :], (t, 2 * t))
        o_ref[0, r] = pltpu.roll(x, t + 1, 1, stride=1, stride_axis=0)[:, :t]


def _bias_tiles(rel_bias, seq):
    t = BIAS_T
    assert t >= MAX_DISTANCE and ATTN_Q % t == 0
    off = jnp.arange(2 * t, dtype=jnp.int32) - (t - 1)
    rel = jnp.stack([off - t, off, off + t, jnp.full_like(off, -seq), jnp.full_like(off, seq)])
    u = rel_bias.astype(F32)[_t5_bucket(rel)].transpose(2, 0, 1) * LOG2E
    u = jnp.roll(u[..., ::-1], -1, axis=-1)
    n_tiles = rel.shape[0]
    return pl.pallas_call(
        _bias_kernel,
        grid=(HEADS,),
        in_specs=[pl.BlockSpec((1, n_tiles, 2 * t), lambda h: (h, 0, 0))],
        out_specs=pl.BlockSpec((1, n_tiles, t, t), lambda h: (h, 0, 0, 0)),
        out_shape=jax.ShapeDtypeStruct((HEADS, n_tiles, t, t), F32),
        compiler_params=pltpu.CompilerParams(dimension_semantics=("arbitrary",)),
        name="bias_tiles",
    )(u)


def _rotary_tables(seq):
    half = QK_DIM // 2
    theta = 1.0 / (ROPE_BASE ** jnp.linspace(0.0, 1.0, half, dtype=F32))
    ang = jnp.arange(seq, dtype=F32)[:, None] * theta[None, :]
    cos = jnp.cos(ang)
    sin = jnp.sin(ang)
    reps = LANES // QK_DIM
    cos_t = jnp.tile(jnp.concatenate([cos, cos], axis=1), (1, reps))
    sin_t = jnp.tile(jnp.concatenate([-sin, sin], axis=1), (1, reps))
    return cos_t, sin_t


def kernel(x, norm_w, w_in, diff_lambda, diff_subln_w, ret_decay_logit, pool_w, pool_scale, w_out,
           rel_bias, final_norm_w):
    batch, seq, d_model = x.shape
    assert d_model == D_MODEL and seq % PROJ_ROWS == 0 and seq % OUT_ROWS == 0
    assert seq % ATTN_Q == 0
    rows = batch * seq
    cos_t, sin_t = _rotary_tables(seq)
    bias_t = _bias_tiles(rel_bias, seq)
    fnw = final_norm_w.reshape(1, D_MODEL).astype(F32)

    h = x.reshape(rows, D_MODEL)
    for l in range(DEPTH):
        lam_init = 0.8 - 0.6 * math.exp(-0.3 * l)
        lf = diff_lambda[l].astype(F32)
        lam = (jnp.exp(jnp.sum(lf[0] * lf[1])) - jnp.exp(jnp.sum(lf[2] * lf[3])) + lam_init)
        log_gamma = jax.nn.log_sigmoid(ret_decay_logit[l].astype(F32))

        aq, ak, av, ga, bq, bk, bv, gb, cu, gc = _proj_call(
            h, norm_w[l].reshape(1, D_MODEL).astype(F32), w_in[l].astype(BF16), cos_t, sin_t, seq)
        to3 = lambda a: a.reshape(batch, seq, a.shape[-1])
        oa = _attn_call(lam.reshape(1), to3(aq), to3(ak), av, to3(ga), bias_t,
                        diff_subln_w[l].reshape(1, V_DIM).astype(F32), lam_init)
        ob = _ret_call(log_gamma, to3(bq), to3(bk), to3(bv), to3(gb))
        h = _out_call(oa.reshape(rows, D_BRANCH), ob.reshape(rows, D_BRANCH), cu, gc, h,
                      w_out[l].astype(BF16), pool_w[l].astype(BF16),
                      pool_scale[l].reshape(1, D_BRANCH).astype(F32), fnw, seq,
                      final=(l == DEPTH - 1))
    return h.reshape(batch, seq, D_MODEL)
```

```python
import functools
import math

import jax
import jax.numpy as jnp
from jax import lax
from jax.experimental import pallas as pl
from jax.experimental.pallas import tpu as pltpu

F32 = jnp.float32
BF16 = jnp.bfloat16

D_MODEL = 1024
DEPTH = 2
D_BRANCH = 512
HEADS = 4
QK_DIM = 64
V_DIM = 128
RET_CHUNK = 128
ROPE_BASE = 10000.0
POOL_WINDOWS = (2, 4, 8, 16)
POOL_HALO = 8
EDGE_ROWS = 16
NUM_BUCKETS = 32
MAX_DISTANCE = 128
EPS = 1e-6
LOG2E = math.log2(math.e)

_SIZES = (512, 512, 512, 512, 256, 256, 512, 512, 512, 512)
_OFFS = tuple(sum(_SIZES[:i]) for i in range(len(_SIZES)))
D_IN = sum(_SIZES)

LANES = 128
VMEM_LIMIT = 56 * 1024 * 1024

PROJ_ROWS = 1024
OUT_ROWS = 1024
RET_HEADS = 4
ATTN_Q = 512
ATTN_K = 256
BIAS_T = 256


def _silu(x):
    return x * jax.nn.sigmoid(x)


def _proj_kernel(x_ref, nw_ref, w_ref, cos_ref, sin_ref,
                 aq_ref, ak_ref, av_ref, ga_ref, bq_ref, bk_ref, bv_ref, gb_ref, cu_ref, gc_ref):
    x = x_ref[...]
    ms = jnp.mean(x * x, axis=-1, keepdims=True)
    h = (x * lax.rsqrt(ms + EPS) * nw_ref[...]).astype(BF16)

    def proj(idx):
        lo = _OFFS[idx]
        return jnp.dot(h, w_ref[:, lo:lo + _SIZES[idx]], preferred_element_type=F32)

    aq_ref[...] = (proj(0) * (QK_DIM ** -0.5 * LOG2E)).astype(BF16)
    ak_ref[...] = proj(1).astype(BF16)
    av_ref[0] = proj(2).T.astype(BF16)
    ga_ref[...] = _silu(proj(3)).astype(BF16)

    cos = cos_ref[...]
    sin = sin_ref[...]
    lane = lax.broadcasted_iota(jnp.int32, cos.shape, 1)
    first_half = (lane % QK_DIM) < (QK_DIM // 2)
    low_head = lane < QK_DIM

    def rotary_dup(t, scale):
        outs = []
        for half in range(2):
            th = t[:, half * LANES:(half + 1) * LANES]
            swapped = jnp.where(first_half, pltpu.roll(th, LANES - QK_DIM // 2, 1),
                                pltpu.roll(th, QK_DIM // 2, 1))
            r = (th * cos + swapped * sin) * scale
            rr = pltpu.roll(r, QK_DIM, 1)
            outs.append(jnp.where(low_head, r, rr))
            outs.append(jnp.where(low_head, rr, r))
        return jnp.concatenate(outs, axis=1).astype(BF16)

    bq_ref[...] = rotary_dup(proj(4), 1.0)
    bk_ref[...] = rotary_dup(proj(5), QK_DIM ** -0.5)
    bv_ref[...] = proj(6).astype(BF16)
    gb_ref[...] = _silu(proj(7)).astype(BF16)
    cu_ref[...] = proj(8)
    gc_ref[...] = _silu(proj(9)).astype(BF16)


def _proj_call(x2, norm_w, w_in, cos_t, sin_t, seq):
    rows = x2.shape[0]
    tm = PROJ_ROWS
    tiles_per_seq = seq // tm
    row_spec = lambda n: pl.BlockSpec((tm, n), lambda i: (i, 0))
    dtypes = [BF16] * 10
    dtypes[8] = F32
    out_shapes = [jax.ShapeDtypeStruct((rows, D_BRANCH), d) for d in dtypes]
    out_specs = [row_spec(D_BRANCH)] * 10
    out_shapes[2] = jax.ShapeDtypeStruct((rows // seq, D_BRANCH, seq), BF16)
    out_specs[2] = pl.BlockSpec((1, D_BRANCH, tm), lambda i: (i // tiles_per_seq, 0,
                                                              i % tiles_per_seq))
    return pl.pallas_call(
        _proj_kernel,
        grid=(rows // tm,),
        in_specs=[
            row_spec(D_MODEL),
            pl.BlockSpec((1, D_MODEL), lambda i: (0, 0)),
            pl.BlockSpec((D_MODEL, D_IN), lambda i: (0, 0), pipeline_mode=pl.Buffered(1)),
            pl.BlockSpec((tm, LANES), lambda i: (i % tiles_per_seq, 0)),
            pl.BlockSpec((tm, LANES), lambda i: (i % tiles_per_seq, 0)),
        ],
        out_specs=out_specs,
        out_shape=out_shapes,
        compiler_params=pltpu.CompilerParams(
            dimension_semantics=("arbitrary",), vmem_limit_bytes=VMEM_LIMIT),
        name="proj",
    )(x2, norm_w, w_in, cos_t, sin_t)


def _attn_kernel(lam_ref, q_ref, k_ref, vt_ref, g_ref, bias_ref, sw_ref, o_ref,
                 s_scr, m_scr, e_scr, c_scr, a_scr, *, seq, nq, lam_init):
    tq = ATTN_Q
    nb = ATTN_K
    j = pl.program_id(0)
    nkb = seq // BIAS_T

    @pl.when(j == 0)
    def _():
        s_scr[...] = jnp.zeros(s_scr.shape, F32)
        m_scr[...] = jnp.zeros(m_scr.shape, F32)
        e_scr[...] = jnp.zeros(e_scr.shape, BF16)
        c_scr[...] = jnp.zeros(c_scr.shape, BF16)
        a_scr[...] = jnp.zeros(a_scr.shape, F32)

    qi = jnp.minimum(j, pl.num_programs(0) - 3) % nq

    c_b = c_scr[0:1, :]
    vt = vt_ref[0]
    nqb = tq // nb
    o_t = jnp.concatenate(
        [jnp.dot(vt, e_scr[n] - c_b[:, n * nb:(n + 1) * nb] * e_scr[nqb + n],
                 preferred_element_type=F32) for n in range(nqb)], axis=1)
    o = (o_t * a_scr[0:1, :]).T
    o = o * lax.rsqrt(jnp.mean(o * o, axis=-1, keepdims=True) + EPS) * sw_ref[...]
    o = o * (1.0 - lam_init)
    o_ref[0] = (o * g_ref[0].astype(F32)).astype(BF16)

    q = q_ref[0]
    lane = lax.broadcasted_iota(jnp.int32, q.shape, 1)
    zero = jnp.zeros_like(q)
    q_st = jnp.concatenate([jnp.where(lane < QK_DIM, q, zero),
                            jnp.where(lane < QK_DIM, zero, q)], axis=0)
    k_all = k_ref[0]
    m_old = m_scr[0:1, :]
    m_new = []
    l_new = []
    for n in range(2 * tq // nb):
        cols = slice(n * nb, (n + 1) * nb)
        e = jnp.exp2(s_scr[n] - m_old[:, cols])
        e_scr[n] = e.astype(BF16)
        l_new.append(jnp.sum(e, axis=0, keepdims=True))
        s = lax.dot_general(k_all, q_st[cols, :], (((1,), (1,)), ((), ())),
                            preferred_element_type=F32)
        qb = qi * (tq // BIAS_T) + (n % (tq // nb)) * (nb // BIAS_T)
        tiles = []
        for c in range(nkb):
            rel = c - qb
            tiles.append(bias_ref[0, jnp.where(rel < -1, 3, jnp.where(rel > 1, 4, rel + 1))])
        s = s + jnp.concatenate(tiles, axis=0)
        s_scr[n] = s
        m_new.append(jnp.max(s, axis=0, keepdims=True))
    m_scr[...] = jnp.broadcast_to(jnp.concatenate(m_new, axis=1), m_scr.shape)

    l_row = jnp.concatenate(l_new, axis=1)
    l1, l2 = l_row[:, :tq], l_row[:, tq:]
    c_scr[...] = jnp.broadcast_to(lam_ref[0] * l1 / l2, c_scr.shape).astype(BF16)
    a_scr[...] = jnp.broadcast_to(1.0 / l1, a_scr.shape)


def _attn_call(lam, aq, ak, avt, ga, bias_t, subln_w, lam_init):
    batch, seq, _ = aq.shape
    tq = ATTN_Q
    nq = seq // tq
    n_tiles = batch * HEADS * nq
    kern = functools.partial(_attn_kernel, seq=seq, nq=nq, lam_init=lam_init)

    def front(j):
        t = jnp.minimum(j, n_tiles - 1)
        return t // (HEADS * nq), (t // nq) % HEADS, t % nq

    def back(j):
        t = jnp.maximum(j - 2, 0)
        return t // (HEADS * nq), (t // nq) % HEADS, t % nq

    def tile_of(where):
        return lambda j: (where(j)[0], where(j)[2], where(j)[1])

    return pl.pallas_call(
        kern,
        grid=(n_tiles + 2,),
        in_specs=[
            pl.BlockSpec(memory_space=pltpu.SMEM),
            pl.BlockSpec((1, tq, LANES), tile_of(front)),
            pl.BlockSpec((1, seq, LANES), lambda j: (front(j)[0], 0, front(j)[1])),
            pl.BlockSpec((1, V_DIM, seq), lambda j: (back(j)[0], back(j)[1], 0)),
            pl.BlockSpec((1, tq, LANES), tile_of(back)),
            pl.BlockSpec((1,) + bias_t.shape[1:], lambda j: (front(j)[1], 0, 0, 0)),
            pl.BlockSpec((1, V_DIM), lambda j: (0, 0)),
        ],
        out_specs=pl.BlockSpec((1, tq, LANES), tile_of(back)),
        out_shape=jax.ShapeDtypeStruct((batch, seq, D_BRANCH), BF16),
        scratch_shapes=[
            pltpu.VMEM((2 * tq // ATTN_K, seq, ATTN_K), F32),
            pltpu.VMEM((8, 2 * tq), F32),
            pltpu.VMEM((2 * tq // ATTN_K, seq, ATTN_K), BF16),
            pltpu.VMEM((16, tq), BF16),
            pltpu.VMEM((8, tq), F32),
        ],
        compiler_params=pltpu.CompilerParams(
            dimension_semantics=("arbitrary",), vmem_limit_bytes=VMEM_LIMIT),
        name="diff_attn",
    )(lam, aq, ak, avt, ga, bias_t, subln_w)


def _ret_kernel(lg_ref, q_ref, k_ref, v_ref, g_ref, o_ref, kv_scr, r_scr, *, seq):
    C = RET_CHUNK
    nc = seq // C
    ri = lax.broadcasted_iota(jnp.int32, (C, LANES), 0).astype(F32)
    fwd_lane = lax.broadcasted_iota(jnp.int32, (C, LANES), 1) < QK_DIM
    dn = (lax.broadcasted_iota(jnp.int32, (C, C), 0)
          - lax.broadcasted_iota(jnp.int32, (C, C), 1)).astype(F32)
    top = lax.broadcasted_iota(jnp.int32, (LANES, V_DIM), 0) < QK_DIM

    for hh in range(RET_HEADS):
        hd = pl.program_id(1) * RET_HEADS + hh
        hl = slice(hh * LANES, (hh + 1) * LANES)
        lgf = lg_ref[0, hd]
        lgb = lg_ref[1, hd]
        q_scale = jnp.exp(jnp.where(fwd_lane, lgf * (ri + 1.0), lgb * (C - ri)))
        k_scale = jnp.exp(jnp.where(fwd_lane, lgf * (C - 1.0 - ri), lgb * ri))
        decay = 0.5 * jnp.where(dn >= 0.0, jnp.exp(lgf * jnp.maximum(dn, 0.0)),
                                jnp.exp(lgb * jnp.maximum(-dn, 0.0)))

        for c in range(nc):
            rows = slice(c * C, (c + 1) * C)
            kk = (k_ref[0, rows, hl].astype(F32) * k_scale).astype(BF16)
            kv_scr[hh * nc + c] = lax.dot_general(kk, v_ref[0, rows, hl], (((0,), (0,)), ((), ())),
                                                  preferred_element_type=F32)

        dec_f = jnp.exp(lgf * C)
        dec_b = jnp.exp(lgb * C)
        state = jnp.zeros((LANES, V_DIM), F32)
        for c in range(nc):
            r_scr[hh * nc + c] = state
            state = state * dec_f + kv_scr[hh * nc + c]
        state = jnp.zeros((LANES, V_DIM), F32)
        for c in reversed(range(nc)):
            r_scr[hh * nc + c] = jnp.where(top, r_scr[hh * nc + c], state)
            state = state * dec_b + kv_scr[hh * nc + c]

        for c in range(nc):
            rows = slice(c * C, (c + 1) * C)
            qc = q_ref[0, rows, hl]
            kc = k_ref[0, rows, hl]
            s = lax.dot_general(qc, kc, (((1,), (1,)), ((), ())), preferred_element_type=F32)
            a = (s * decay).astype(BF16)
            qq = (qc.astype(F32) * q_scale).astype(BF16)
            o = (jnp.dot(a, v_ref[0, rows, hl], preferred_element_type=F32)
                 + jnp.dot(qq, r_scr[hh * nc + c].astype(BF16), preferred_element_type=F32))
            o = o * lax.rsqrt(jnp.mean(o * o, axis=-1, keepdims=True) + EPS)
            o_ref[0, rows, hl] = (o * g_ref[0, rows, hl].astype(F32)).astype(BF16)


def _ret_call(log_gamma, bq, bk, bv, gb):
    batch, seq, _ = bq.shape
    nc = seq // RET_CHUNK
    spec = pl.BlockSpec((1, seq, RET_HEADS * LANES), lambda b, h: (b, 0, h))
    state_shape = (RET_HEADS * nc, LANES, V_DIM)
    return pl.pallas_call(
        functools.partial(_ret_kernel, seq=seq),
        grid=(batch, HEADS // RET_HEADS),
        in_specs=[pl.BlockSpec(memory_space=pltpu.SMEM), spec, spec, spec, spec],
        out_specs=spec,
        out_shape=jax.ShapeDtypeStruct((batch, seq, D_BRANCH), BF16),
        scratch_shapes=[pltpu.VMEM(state_shape, F32), pltpu.VMEM(state_shape, F32)],
        compiler_params=pltpu.CompilerParams(
            dimension_semantics=("arbitrary", "arbitrary"), vmem_limit_bytes=VMEM_LIMIT),
        name="retention",
    )(log_gamma, bq, bk, bv, gb)


def _out_kernel(oa_ref, ob_ref, cu_ref, cup_ref, cun_ref, gc_ref, x_ref, wo_ref, pw_ref, ps_ref,
                fnw_ref, o_ref, ext_scr, pooled_scr, *, seq, final):
    tm = cu_ref.shape[0]
    tiles_per_seq = seq // tm
    it = pl.program_id(0) % tiles_per_seq

    ext_scr[0:POOL_HALO, :] = jnp.where(it == 0, 0.0, cup_ref[...])
    ext_scr[POOL_HALO:POOL_HALO + tm, :] = cu_ref[...]
    ext_scr[POOL_HALO + tm:, :] = jnp.where(it == tiles_per_seq - 1, 0.0, cun_ref[...])

    n = tm + 2 * POOL_HALO
    body = slice(POOL_HALO, POOL_HALO + tm)
    for g, w in enumerate(POOL_WINDOWS):
        cols = slice(g * LANES, (g + 1) * LANES)
        run = {1: ext_scr[:, cols]}
        k = 1
        while 2 * k <= min(w, POOL_HALO):
            run[2 * k] = run[k] + pltpu.roll(run[k], n - k, 0)
            k *= 2
        if w == 2 * POOL_HALO:
            wsum = run[POOL_HALO][0:tm] + run[POOL_HALO][body]
        else:
            wsum = pltpu.roll(run[w], w // 2, 0)[body]
        u = cu_ref[:, cols]
        pooled_scr[:, cols] = (wsum * (1.0 / w) - u).astype(BF16)
        for r0 in (0, tm - EDGE_ROWS):
            edge = slice(r0, r0 + EDGE_ROWS)
            pos = it * tm + r0 + lax.broadcasted_iota(jnp.int32, (EDGE_ROWS, 1), 0)
            lo = jnp.maximum(pos - w // 2, 0)
            hi = jnp.minimum(pos + (w - w // 2), seq)
            count = (hi - lo).astype(F32)
            pooled_scr[edge, cols] = (wsum[edge] / count - u[edge]).astype(BF16)

    out = x_ref[...] + jnp.dot(oa_ref[...], wo_ref[0:D_BRANCH, :], preferred_element_type=F32)
    out = out + jnp.dot(ob_ref[...], wo_ref[D_BRANCH:2 * D_BRANCH, :],
                        preferred_element_type=F32)
    ys = [jnp.dot(pooled_scr[:, g * LANES:(g + 1) * LANES], pw_ref[g],
                  preferred_element_type=F32) for g in range(len(POOL_WINDOWS))]
    yc = jnp.concatenate(ys, axis=1) * ps_ref[...] * gc_ref[...].astype(F32)
    out = out + jnp.dot(yc.astype(BF16), wo_ref[2 * D_BRANCH:, :], preferred_element_type=F32)
    if final:
        out = out * lax.rsqrt(jnp.mean(out * out, axis=-1, keepdims=True) + EPS) * fnw_ref[...]
    o_ref[...] = out


def _out_call(oa, ob, cu, gc, x2, w_out, pool_w, pool_scale, final_norm_w, seq, final):
    rows = x2.shape[0]
    tm = OUT_ROWS
    halo_blocks = tm // POOL_HALO
    last_halo = rows // POOL_HALO - 1
    row_spec = lambda n: pl.BlockSpec((tm, n), lambda i: (i, 0))
    const = lambda shape: pl.BlockSpec(shape, lambda i: (0,) * len(shape))
    return pl.pallas_call(
        functools.partial(_out_kernel, seq=seq, final=final),
        grid=(rows // tm,),
        in_specs=[
            row_spec(D_BRANCH), row_spec(D_BRANCH), row_spec(D_BRANCH),
            pl.BlockSpec((POOL_HALO, D_BRANCH),
                         lambda i: (jnp.maximum(i * halo_blocks - 1, 0), 0)),
            pl.BlockSpec((POOL_HALO, D_BRANCH),
                         lambda i: (jnp.minimum((i + 1) * halo_blocks, last_halo), 0)),
            row_spec(D_BRANCH), row_spec(D_MODEL),
            const((3 * D_BRANCH, D_MODEL)),
            const((len(POOL_WINDOWS), LANES, LANES)),
            const((1, D_BRANCH)),
            const((1, D_MODEL)),
        ],
        out_specs=row_spec(D_MODEL),
        out_shape=jax.ShapeDtypeStruct((rows, D_MODEL), F32),
        scratch_shapes=[pltpu.VMEM((tm + 2 * POOL_HALO, D_BRANCH), F32),
                        pltpu.VMEM((tm, D_BRANCH), BF16)],
        compiler_params=pltpu.CompilerParams(
            dimension_semantics=("arbitrary",), vmem_limit_bytes=VMEM_LIMIT),
        name="pool_out",
    )(oa, ob, cu, cu, cu, gc, x2, w_out, pool_w, pool_scale, final_norm_w)


def _t5_bucket(rel):
    half = NUM_BUCKETS // 2
    max_exact = half // 2
    ret = jnp.where(rel > 0, half, 0)
    n = jnp.abs(rel)
    nf = jnp.maximum(n, 1).astype(F32)
    large = max_exact + (jnp.log(nf / max_exact) / math.log(MAX_DISTANCE / max_exact)
                         * (half - max_exact)).astype(jnp.int32)
    large = jnp.minimum(large, half - 1)
    return ret + jnp.where(n < max_exact, n, large)


def _bias_kernel(u_ref, o_ref):
    t = BIAS_T
    for r in range(u_ref.shape[1]):
        x = jnp.broadcast_to(u_ref[0, r:r + 1, :], (t, 2 * t))
        o_ref[0, r] = pltpu.roll(x, t + 1, 1, stride=1, stride_axis=0)[:, :t]


def _bias_tiles(rel_bias, seq):
    t = BIAS_T
    assert t >= MAX_DISTANCE and ATTN_Q % t == 0
    off = jnp.arange(2 * t, dtype=jnp.int32) - (t - 1)
    rel = jnp.stack([off - t, off, off + t, jnp.full_like(off, -seq), jnp.full_like(off, seq)])
    u = rel_bias.astype(F32)[_t5_bucket(rel)].transpose(2, 0, 1) * LOG2E
    u = jnp.roll(u[..., ::-1], -1, axis=-1)
    n_tiles = rel.shape[0]
    return pl.pallas_call(
        _bias_kernel,
        grid=(HEADS,),
        in_specs=[pl.BlockSpec((1, n_tiles, 2 * t), lambda h: (h, 0, 0))],
        out_specs=pl.BlockSpec((1, n_tiles, t, t), lambda h: (h, 0, 0, 0)),
        out_shape=jax.ShapeDtypeStruct((HEADS, n_tiles, t, t), F32),
        compiler_params=pltpu.CompilerParams(dimension_semantics=("arbitrary",)),
        name="bias_tiles",
    )(u)


def _rotary_tables(seq):
    half = QK_DIM // 2
    theta = 1.0 / (ROPE_BASE ** jnp.linspace(0.0, 1.0, half, dtype=F32))
    ang = jnp.arange(seq, dtype=F32)[:, None] * theta[None, :]
    cos = jnp.cos(ang)
    sin = jnp.sin(ang)
    reps = LANES // QK_DIM
    cos_t = jnp.tile(jnp.concatenate([cos, cos], axis=1), (1, reps))
    sin_t = jnp.tile(jnp.concatenate([-sin, sin], axis=1), (1, reps))
    return cos_t, sin_t


def kernel(x, norm_w, w_in, diff_lambda, diff_subln_w, ret_decay_logit, pool_w, pool_scale, w_out,
           rel_bias, final_norm_w):
    batch, seq, d_model = x.shape
    assert d_model == D_MODEL and seq % PROJ_ROWS == 0 and seq % OUT_ROWS == 0
    assert seq % ATTN_Q == 0
    rows = batch * seq
    cos_t, sin_t = _rotary_tables(seq)
    bias_t = _bias_tiles(rel_bias, seq)
    fnw = final_norm_w.reshape(1, D_MODEL).astype(F32)

    h = x.reshape(rows, D_MODEL)
    for l in range(DEPTH):
        lam_init = 0.8 - 0.6 * math.exp(-0.3 * l)
        lf = diff_lambda[l].astype(F32)
        lam = (jnp.exp(jnp.sum(lf[0] * lf[1])) - jnp.exp(jnp.sum(lf[2] * lf[3])) + lam_init)
        log_gamma = jax.nn.log_sigmoid(ret_decay_logit[l].astype(F32))

        aq, ak, av, ga, bq, bk, bv, gb, cu, gc = _proj_call(
            h, norm_w[l].reshape(1, D_MODEL).astype(F32), w_in[l].astype(BF16), cos_t, sin_t, seq)
        to3 = lambda a: a.reshape(batch, seq, a.shape[-1])
        oa = _attn_call(lam.reshape(1), to3(aq), to3(ak), av, to3(ga), bias_t,
                        diff_subln_w[l].reshape(1, V_DIM).astype(F32), lam_init)
        ob = _ret_call(log_gamma, to3(bq), to3(bk), to3(bv), to3(gb))
        h = _out_call(oa.reshape(rows, D_BRANCH), ob.reshape(rows, D_BRANCH), cu, gc, h,
                      w_out[l].astype(BF16), pool_w[l].astype(BF16),
                      pool_scale[l].reshape(1, D_BRANCH).astype(F32), fnw, seq,
                      final=(l == DEPTH - 1))
    return h.reshape(batch, seq, D_MODEL)
```

```python
import functools
import math

import jax
import jax.numpy as jnp
from jax import lax
from jax.experimental import pallas as pl
from jax.experimental.pallas import tpu as pltpu

F32 = jnp.float32
BF16 = jnp.bfloat16

D_MODEL = 1024
DEPTH = 2
D_BRANCH = 512
HEADS = 4
QK_DIM = 64
V_DIM = 128
RET_CHUNK = 128
ROPE_BASE = 10000.0
POOL_WINDOWS = (2, 4, 8, 16)
POOL_HALO = 8
EDGE_ROWS = 16
NUM_BUCKETS = 32
MAX_DISTANCE = 128
EPS = 1e-6
LOG2E = math.log2(math.e)

_SIZES = (512, 512, 512, 512, 256, 256, 512, 512, 512, 512)
_OFFS = tuple(sum(_SIZES[:i]) for i in range(len(_SIZES)))
D_IN = sum(_SIZES)

LANES = 128
VMEM_LIMIT = 56 * 1024 * 1024

PROJ_ROWS = 1024
OUT_ROWS = 1024
RET_HEADS = 4
ATTN_Q = 512
ATTN_SUB = 2
ATTN_K = 256
BIAS_T = 256


def _silu(x):
    return x * jax.nn.sigmoid(x)


def _proj_kernel(x_ref, nw_ref, w_ref, cos_ref, sin_ref,
                 aq_ref, ak_ref, av_ref, ga_ref, bq_ref, bk_ref, bv_ref, gb_ref, cu_ref, gc_ref):
    x = x_ref[...]
    ms = jnp.mean(x * x, axis=-1, keepdims=True)
    h = (x * lax.rsqrt(ms + EPS) * nw_ref[...]).astype(BF16)

    def proj(idx):
        lo = _OFFS[idx]
        return jnp.dot(h, w_ref[:, lo:lo + _SIZES[idx]], preferred_element_type=F32)

    aq_ref[...] = (proj(0) * (QK_DIM ** -0.5 * LOG2E)).astype(BF16)
    ak_ref[...] = proj(1).astype(BF16)
    av_ref[0] = proj(2).T.astype(BF16)
    ga_ref[...] = _silu(proj(3)).astype(BF16)

    cos = cos_ref[...]
    sin = sin_ref[...]
    lane = lax.broadcasted_iota(jnp.int32, cos.shape, 1)
    first_half = (lane % QK_DIM) < (QK_DIM // 2)
    low_head = lane < QK_DIM

    def rotary_dup(t, scale):
        outs = []
        for half in range(2):
            th = t[:, half * LANES:(half + 1) * LANES]
            swapped = jnp.where(first_half, pltpu.roll(th, LANES - QK_DIM // 2, 1),
                                pltpu.roll(th, QK_DIM // 2, 1))
            r = (th * cos + swapped * sin) * scale
            rr = pltpu.roll(r, QK_DIM, 1)
            outs.append(jnp.where(low_head, r, rr))
            outs.append(jnp.where(low_head, rr, r))
        return jnp.concatenate(outs, axis=1).astype(BF16)

    bq_ref[...] = rotary_dup(proj(4), 1.0)
    bk_ref[...] = rotary_dup(proj(5), QK_DIM ** -0.5)
    bv_ref[...] = proj(6).astype(BF16)
    gb_ref[...] = _silu(proj(7)).astype(BF16)
    cu_ref[...] = proj(8)
    gc_ref[...] = _silu(proj(9)).astype(BF16)


def _proj_call(x2, norm_w, w_in, cos_t, sin_t, seq):
    rows = x2.shape[0]
    tm = PROJ_ROWS
    tiles_per_seq = seq // tm
    row_spec = lambda n: pl.BlockSpec((tm, n), lambda i: (i, 0))
    dtypes = [BF16] * 10
    dtypes[8] = F32
    out_shapes = [jax.ShapeDtypeStruct((rows, D_BRANCH), d) for d in dtypes]
    out_specs = [row_spec(D_BRANCH)] * 10
    out_shapes[2] = jax.ShapeDtypeStruct((rows // seq, D_BRANCH, seq), BF16)
    out_specs[2] = pl.BlockSpec((1, D_BRANCH, tm), lambda i: (i // tiles_per_seq, 0,
                                                              i % tiles_per_seq))
    return pl.pallas_call(
        _proj_kernel,
        grid=(rows // tm,),
        in_specs=[
            row_spec(D_MODEL),
            pl.BlockSpec((1, D_MODEL), lambda i: (0, 0)),
            pl.BlockSpec((D_MODEL, D_IN), lambda i: (0, 0), pipeline_mode=pl.Buffered(1)),
            pl.BlockSpec((tm, LANES), lambda i: (i % tiles_per_seq, 0)),
            pl.BlockSpec((tm, LANES), lambda i: (i % tiles_per_seq, 0)),
        ],
        out_specs=out_specs,
        out_shape=out_shapes,
        compiler_params=pltpu.CompilerParams(
            dimension_semantics=("arbitrary",), vmem_limit_bytes=VMEM_LIMIT),
        name="proj",
    )(x2, norm_w, w_in, cos_t, sin_t)


def _attn_kernel(lam_ref, q_ref, k_ref, vt_ref, g_ref, bias_ref, sw_ref, o_ref,
                 s_scr, m_scr, e_scr, c_scr, a_scr, *, seq, nq, lam_init):
    tq = ATTN_Q
    nb = ATTN_K
    j = pl.program_id(0)
    nkb = seq // BIAS_T

    @pl.when(j == 0)
    def _():
        s_scr[...] = jnp.zeros(s_scr.shape, F32)
        m_scr[...] = jnp.zeros(m_scr.shape, F32)
        e_scr[...] = jnp.zeros(e_scr.shape, BF16)
        c_scr[...] = jnp.zeros(c_scr.shape, BF16)
        a_scr[...] = jnp.zeros(a_scr.shape, F32)

    step_in_seq = jnp.minimum(j, pl.num_programs(0) - 3) % nq
    vt = vt_ref[0]
    k_all = k_ref[0]
    nqb = tq // nb
    ncb = 2 * nqb
    lane = lax.broadcasted_iota(jnp.int32, (tq, LANES), 1)

    for sub in range(ATTN_SUB):
        rows = slice(sub * tq, (sub + 1) * tq)
        qi = step_in_seq * ATTN_SUB + sub

        c_b = c_scr[sub, 0:1, :]
        o_t = jnp.concatenate(
            [jnp.dot(vt, e_scr[sub * ncb + n]
                     - c_b[:, n * nb:(n + 1) * nb] * e_scr[sub * ncb + nqb + n],
                     preferred_element_type=F32) for n in range(nqb)], axis=1)
        o = (o_t * a_scr[sub, 0:1, :]).T
        o = o * lax.rsqrt(jnp.mean(o * o, axis=-1, keepdims=True) + EPS) * sw_ref[...]
        o = o * (1.0 - lam_init)
        o_ref[0, rows, :] = (o * g_ref[0, rows, :].astype(F32)).astype(BF16)

        q = q_ref[0, rows, :]
        zero = jnp.zeros_like(q)
        q_st = jnp.concatenate([jnp.where(lane < QK_DIM, q, zero),
                                jnp.where(lane < QK_DIM, zero, q)], axis=0)
        m_old = m_scr[sub, 0:1, :]
        m_new = []
        l_new = []
        for n in range(ncb):
            cols = slice(n * nb, (n + 1) * nb)
            e = jnp.exp2(s_scr[sub * ncb + n] - m_old[:, cols])
            e_scr[sub * ncb + n] = e.astype(BF16)
            l_new.append(jnp.sum(e, axis=0, keepdims=True))
            s = lax.dot_general(k_all, q_st[cols, :], (((1,), (1,)), ((), ())),
                                preferred_element_type=F32)
            qb = qi * (tq // BIAS_T) + (n % nqb) * (nb // BIAS_T)
            tiles = []
            for c in range(nkb):
                rel = c - qb
                tiles.append(
                    bias_ref[0, jnp.where(rel < -1, 3, jnp.where(rel > 1, 4, rel + 1))])
            s = s + jnp.concatenate(tiles, axis=0)
            s_scr[sub * ncb + n] = s
            m_new.append(jnp.max(s, axis=0, keepdims=True))
        m_scr[sub] = jnp.broadcast_to(jnp.concatenate(m_new, axis=1), m_scr.shape[1:])

        l_row = jnp.concatenate(l_new, axis=1)
        l1, l2 = l_row[:, :tq], l_row[:, tq:]
        c_scr[sub] = jnp.broadcast_to(lam_ref[0] * l1 / l2, c_scr.shape[1:]).astype(BF16)
        a_scr[sub] = jnp.broadcast_to(1.0 / l1, a_scr.shape[1:])


def _attn_call(lam, aq, ak, avt, ga, bias_t, subln_w, lam_init):
    batch, seq, _ = aq.shape
    tq = ATTN_Q * ATTN_SUB
    nq = seq // tq
    n_tiles = batch * HEADS * nq
    n_slabs = ATTN_SUB * 2 * ATTN_Q // ATTN_K
    kern = functools.partial(_attn_kernel, seq=seq, nq=nq, lam_init=lam_init)

    def front(j):
        t = jnp.minimum(j, n_tiles - 1)
        return t // (HEADS * nq), (t // nq) % HEADS, t % nq

    def back(j):
        t = jnp.maximum(j - 2, 0)
        return t // (HEADS * nq), (t // nq) % HEADS, t % nq

    def tile_of(where):
        return lambda j: (where(j)[0], where(j)[2], where(j)[1])

    return pl.pallas_call(
        kern,
        grid=(n_tiles + 2,),
        in_specs=[
            pl.BlockSpec(memory_space=pltpu.SMEM),
            pl.BlockSpec((1, tq, LANES), tile_of(front)),
            pl.BlockSpec((1, seq, LANES), lambda j: (front(j)[0], 0, front(j)[1])),
            pl.BlockSpec((1, V_DIM, seq), lambda j: (back(j)[0], back(j)[1], 0)),
            pl.BlockSpec((1, tq, LANES), tile_of(back)),
            pl.BlockSpec((1,) + bias_t.shape[1:], lambda j: (front(j)[1], 0, 0, 0)),
            pl.BlockSpec((1, V_DIM), lambda j: (0, 0)),
        ],
        out_specs=pl.BlockSpec((1, tq, LANES), tile_of(back)),
        out_shape=jax.ShapeDtypeStruct((batch, seq, D_BRANCH), BF16),
        scratch_shapes=[
            pltpu.VMEM((n_slabs, seq, ATTN_K), F32),
            pltpu.VMEM((ATTN_SUB, 8, 2 * ATTN_Q), F32),
            pltpu.VMEM((n_slabs, seq, ATTN_K), BF16),
            pltpu.VMEM((ATTN_SUB, 16, ATTN_Q), BF16),
            pltpu.VMEM((ATTN_SUB, 8, ATTN_Q), F32),
        ],
        compiler_params=pltpu.CompilerParams(
            dimension_semantics=("arbitrary",), vmem_limit_bytes=VMEM_LIMIT),
        name="diff_attn",
    )(lam, aq, ak, avt, ga, bias_t, subln_w)


def _ret_kernel(lg_ref, q_ref, k_ref, v_ref, g_ref, o_ref, kv_scr, r_scr, *, seq):
    C = RET_CHUNK
    nc = seq // C
    ri = lax.broadcasted_iota(jnp.int32, (C, LANES), 0).astype(F32)
    fwd_lane = lax.broadcasted_iota(jnp.int32, (C, LANES), 1) < QK_DIM
    dn = (lax.broadcasted_iota(jnp.int32, (C, C), 0)
          - lax.broadcasted_iota(jnp.int32, (C, C), 1)).astype(F32)
    top = lax.broadcasted_iota(jnp.int32, (LANES, V_DIM), 0) < QK_DIM

    for hh in range(RET_HEADS):
        hd = pl.program_id(1) * RET_HEADS + hh
        hl = slice(hh * LANES, (hh + 1) * LANES)
        lgf = lg_ref[0, hd]
        lgb = lg_ref[1, hd]
        q_scale = jnp.exp(jnp.where(fwd_lane, lgf * (ri + 1.0), lgb * (C - ri)))
        k_scale = jnp.exp(jnp.where(fwd_lane, lgf * (C - 1.0 - ri), lgb * ri))
        decay = 0.5 * jnp.where(dn >= 0.0, jnp.exp(lgf * jnp.maximum(dn, 0.0)),
                                jnp.exp(lgb * jnp.maximum(-dn, 0.0)))

        for c in range(nc):
            rows = slice(c * C, (c + 1) * C)
            kk = (k_ref[0, rows, hl].astype(F32) * k_scale).astype(BF16)
            kv_scr[hh * nc + c] = lax.dot_general(kk, v_ref[0, rows, hl], (((0,), (0,)), ((), ())),
                                                  preferred_element_type=F32)

        dec_f = jnp.exp(lgf * C)
        dec_b = jnp.exp(lgb * C)
        state = jnp.zeros((LANES, V_DIM), F32)
        for c in range(nc):
            r_scr[hh * nc + c] = state
            state = state * dec_f + kv_scr[hh * nc + c]
        state = jnp.zeros((LANES, V_DIM), F32)
        for c in reversed(range(nc)):
            r_scr[hh * nc + c] = jnp.where(top, r_scr[hh * nc + c], state)
            state = state * dec_b + kv_scr[hh * nc + c]

        for c in range(nc):
            rows = slice(c * C, (c + 1) * C)
            qc = q_ref[0, rows, hl]
            kc = k_ref[0, rows, hl]
            s = lax.dot_general(qc, kc, (((1,), (1,)), ((), ())), preferred_element_type=F32)
            a = (s * decay).astype(BF16)
            qq = (qc.astype(F32) * q_scale).astype(BF16)
            o = (jnp.dot(a, v_ref[0, rows, hl], preferred_element_type=F32)
                 + jnp.dot(qq, r_scr[hh * nc + c].astype(BF16), preferred_element_type=F32))
            o = o * lax.rsqrt(jnp.mean(o * o, axis=-1, keepdims=True) + EPS)
            o_ref[0, rows, hl] = (o * g_ref[0, rows, hl].astype(F32)).astype(BF16)


def _ret_call(log_gamma, bq, bk, bv, gb):
    batch, seq, _ = bq.shape
    nc = seq // RET_CHUNK
    spec = pl.BlockSpec((1, seq, RET_HEADS * LANES), lambda b, h: (b, 0, h))
    state_shape = (RET_HEADS * nc, LANES, V_DIM)
    return pl.pallas_call(
        functools.partial(_ret_kernel, seq=seq),
        grid=(batch, HEADS // RET_HEADS),
        in_specs=[pl.BlockSpec(memory_space=pltpu.SMEM), spec, spec, spec, spec],
        out_specs=spec,
        out_shape=jax.ShapeDtypeStruct((batch, seq, D_BRANCH), BF16),
        scratch_shapes=[pltpu.VMEM(state_shape, F32), pltpu.VMEM(state_shape, F32)],
        compiler_params=pltpu.CompilerParams(
            dimension_semantics=("arbitrary", "arbitrary"), vmem_limit_bytes=VMEM_LIMIT),
        name="retention",
    )(log_gamma, bq, bk, bv, gb)


def _out_kernel(oa_ref, ob_ref, cu_ref, cup_ref, cun_ref, gc_ref, x_ref, wo_ref, pw_ref, ps_ref,
                fnw_ref, o_ref, ext_scr, pooled_scr, *, seq, final):
    tm = cu_ref.shape[0]
    tiles_per_seq = seq // tm
    it = pl.program_id(0) % tiles_per_seq

    ext_scr[0:POOL_HALO, :] = jnp.where(it == 0, 0.0, cup_ref[...])
    ext_scr[POOL_HALO:POOL_HALO + tm, :] = cu_ref[...]
    ext_scr[POOL_HALO + tm:, :] = jnp.where(it == tiles_per_seq - 1, 0.0, cun_ref[...])

    n = tm + 2 * POOL_HALO
    body = slice(POOL_HALO, POOL_HALO + tm)
    for g, w in enumerate(POOL_WINDOWS):
        cols = slice(g * LANES, (g + 1) * LANES)
        run = {1: ext_scr[:, cols]}
        k = 1
        while 2 * k <= min(w, POOL_HALO):
            run[2 * k] = run[k] + pltpu.roll(run[k], n - k, 0)
            k *= 2
        if w == 2 * POOL_HALO:
            wsum = run[POOL_HALO][0:tm] + run[POOL_HALO][body]
        else:
            wsum = pltpu.roll(run[w], w // 2, 0)[body]
        u = cu_ref[:, cols]
        pooled_scr[:, cols] = (wsum * (1.0 / w) - u).astype(BF16)
        for r0 in (0, tm - EDGE_ROWS):
            edge = slice(r0, r0 + EDGE_ROWS)
            pos = it * tm + r0 + lax.broadcasted_iota(jnp.int32, (EDGE_ROWS, 1), 0)
            lo = jnp.maximum(pos - w // 2, 0)
            hi = jnp.minimum(pos + (w - w // 2), seq)
            count = (hi - lo).astype(F32)
            pooled_scr[edge, cols] = (wsum[edge] / count - u[edge]).astype(BF16)

    out = x_ref[...] + jnp.dot(oa_ref[...], wo_ref[0:D_BRANCH, :], preferred_element_type=F32)
    out = out + jnp.dot(ob_ref[...], wo_ref[D_BRANCH:2 * D_BRANCH, :],
                        preferred_element_type=F32)
    ys = [jnp.dot(pooled_scr[:, g * LANES:(g + 1) * LANES], pw_ref[g],
                  preferred_element_type=F32) for g in range(len(POOL_WINDOWS))]
    yc = jnp.concatenate(ys, axis=1) * ps_ref[...] * gc_ref[...].astype(F32)
    out = out + jnp.dot(yc.astype(BF16), wo_ref[2 * D_BRANCH:, :], preferred_element_type=F32)
    if final:
        out = out * lax.rsqrt(jnp.mean(out * out, axis=-1, keepdims=True) + EPS) * fnw_ref[...]
    o_ref[...] = out


def _out_call(oa, ob, cu, gc, x2, w_out, pool_w, pool_scale, final_norm_w, seq, final):
    rows = x2.shape[0]
    tm = OUT_ROWS
    halo_blocks = tm // POOL_HALO
    last_halo = rows // POOL_HALO - 1
    row_spec = lambda n: pl.BlockSpec((tm, n), lambda i: (i, 0))
    const = lambda shape: pl.BlockSpec(shape, lambda i: (0,) * len(shape))
    return pl.pallas_call(
        functools.partial(_out_kernel, seq=seq, final=final),
        grid=(rows // tm,),
        in_specs=[
            row_spec(D_BRANCH), row_spec(D_BRANCH), row_spec(D_BRANCH),
            pl.BlockSpec((POOL_HALO, D_BRANCH),
                         lambda i: (jnp.maximum(i * halo_blocks - 1, 0), 0)),
            pl.BlockSpec((POOL_HALO, D_BRANCH),
                         lambda i: (jnp.minimum((i + 1) * halo_blocks, last_halo), 0)),
            row_spec(D_BRANCH), row_spec(D_MODEL),
            const((3 * D_BRANCH, D_MODEL)),
            const((len(POOL_WINDOWS), LANES, LANES)),
            const((1, D_BRANCH)),
            const((1, D_MODEL)),
        ],
        out_specs=row_spec(D_MODEL),
        out_shape=jax.ShapeDtypeStruct((rows, D_MODEL), F32),
        scratch_shapes=[pltpu.VMEM((tm + 2 * POOL_HALO, D_BRANCH), F32),
                        pltpu.VMEM((tm, D_BRANCH), BF16)],
        compiler_params=pltpu.CompilerParams(
            dimension_semantics=("arbitrary",), vmem_limit_bytes=VMEM_LIMIT),
        name="pool_out",
    )(oa, ob, cu, cu, cu, gc, x2, w_out, pool_w, pool_scale, final_norm_w)


def _t5_bucket(rel):
    half = NUM_BUCKETS // 2
    max_exact = half // 2
    ret = jnp.where(rel > 0, half, 0)
    n = jnp.abs(rel)
    nf = jnp.maximum(n, 1).astype(F32)
    large = max_exact + (jnp.log(nf / max_exact) / math.log(MAX_DISTANCE / max_exact)
                         * (half - max_exact)).astype(jnp.int32)
    large = jnp.minimum(large, half - 1)
    return ret + jnp.where(n < max_exact, n, large)


def _bias_kernel(u_ref, o_ref):
    t = BIAS_T
    for r in range(u_ref.shape[1]):
        x = jnp.broadcast_to(u_ref[0, r:r + 1, :], (t, 2 * t))
        o_ref[0, r] = pltpu.roll(x, t + 1, 1, stride=1, stride_axis=0)[:, :t]


def _bias_tiles(rel_bias, seq):
    t = BIAS_T
    assert t >= MAX_DISTANCE and ATTN_Q % t == 0
    off = jnp.arange(2 * t, dtype=jnp.int32) - (t - 1)
    rel = jnp.stack([off - t, off, off + t, jnp.full_like(off, -seq), jnp.full_like(off, seq)])
    u = rel_bias.astype(F32)[_t5_bucket(rel)].transpose(2, 0, 1) * LOG2E
    u = jnp.roll(u[..., ::-1], -1, axis=-1)
    n_tiles = rel.shape[0]
    return pl.pallas_call(
        _bias_kernel,
        grid=(HEADS,),
        in_specs=[pl.BlockSpec((1, n_tiles, 2 * t), lambda h: (h, 0, 0))],
        out_specs=pl.BlockSpec((1, n_tiles, t, t), lambda h: (h, 0, 0, 0)),
        out_shape=jax.ShapeDtypeStruct((HEADS, n_tiles, t, t), F32),
        compiler_params=pltpu.CompilerParams(dimension_semantics=("arbitrary",)),
        name="bias_tiles",
    )(u)


def _rotary_tables(seq):
    half = QK_DIM // 2
    theta = 1.0 / (ROPE_BASE ** jnp.linspace(0.0, 1.0, half, dtype=F32))
    ang = jnp.arange(seq, dtype=F32)[:, None] * theta[None, :]
    cos = jnp.cos(ang)
    sin = jnp.sin(ang)
    reps = LANES // QK_DIM
    cos_t = jnp.tile(jnp.concatenate([cos, cos], axis=1), (1, reps))
    sin_t = jnp.tile(jnp.concatenate([-sin, sin], axis=1), (1, reps))
    return cos_t, sin_t


def kernel(x, norm_w, w_in, diff_lambda, diff_subln_w, ret_decay_logit, pool_w, pool_scale, w_out,
           rel_bias, final_norm_w):
    batch, seq, d_model = x.shape
    assert d_model == D_MODEL and seq % PROJ_ROWS == 0 and seq % OUT_ROWS == 0
    assert seq % (ATTN_Q * ATTN_SUB) == 0
    rows = batch * seq
    cos_t, sin_t = _rotary_tables(seq)
    bias_t = _bias_tiles(rel_bias, seq)
    fnw = final_norm_w.reshape(1, D_MODEL).astype(F32)

    h = x.reshape(rows, D_MODEL)
    for l in range(DEPTH):
        lam_init = 0.8 - 0.6 * math.exp(-0.3 * l)
        lf = diff_lambda[l].astype(F32)
        lam = (jnp.exp(jnp.sum(lf[0] * lf[1])) - jnp.exp(jnp.sum(lf[2] * lf[3])) + lam_init)
        log_gamma = jax.nn.log_sigmoid(ret_decay_logit[l].astype(F32))

        aq, ak, av, ga, bq, bk, bv, gb, cu, gc = _proj_call(
            h, norm_w[l].reshape(1, D_MODEL).astype(F32), w_in[l].astype(BF16), cos_t, sin_t, seq)
        to3 = lambda a: a.reshape(batch, seq, a.shape[-1])
        oa = _attn_call(lam.reshape(1), to3(aq), to3(ak), av, to3(ga), bias_t,
                        diff_subln_w[l].reshape(1, V_DIM).astype(F32), lam_init)
        ob = _ret_call(log_gamma, to3(bq), to3(bk), to3(bv), to3(gb))
        h = _out_call(oa.reshape(rows, D_BRANCH), ob.reshape(rows, D_BRANCH), cu, gc, h,
                      w_out[l].astype(BF16), pool_w[l].astype(BF16),
                      pool_scale[l].reshape(1, D_BRANCH).astype(F32), fnw, seq,
                      final=(l == DEPTH - 1))
    return h.reshape(batch, seq, D_MODEL)
```

```python
import functools
import math

import jax
import jax.numpy as jnp
from jax import lax
from jax.experimental import pallas as pl
from jax.experimental.pallas import tpu as pltpu

F32 = jnp.float32
BF16 = jnp.bfloat16

D_MODEL = 1024
DEPTH = 2
D_BRANCH = 512
HEADS = 4
QK_DIM = 64
V_DIM = 128
RET_CHUNK = 128
ROPE_BASE = 10000.0
POOL_WINDOWS = (2, 4, 8, 16)
POOL_HALO = 8
EDGE_ROWS = 16
NUM_BUCKETS = 32
MAX_DISTANCE = 128
EPS = 1e-6
LOG2E = math.log2(math.e)

_SIZES = (512, 512, 512, 512, 256, 256, 512, 512, 512, 512)
_OFFS = tuple(sum(_SIZES[:i]) for i in range(len(_SIZES)))
D_IN = sum(_SIZES)

LANES = 128
VMEM_LIMIT = 56 * 1024 * 1024

PROJ_ROWS = 1024
OUT_ROWS = 1024
RET_HEADS = 4
ATTN_Q = 512
ATTN_SUB = 2
M_ROWS = 8
ATTN_K = 256
BIAS_T = 256


def _silu(x):
    return x * jax.nn.sigmoid(x)


def _proj_kernel(x_ref, nw_ref, w_ref, cos_ref, sin_ref,
                 aq_ref, ak_ref, av_ref, ga_ref, bq_ref, bk_ref, bv_ref, gb_ref, cu_ref, gc_ref):
    x = x_ref[...]
    ms = jnp.mean(x * x, axis=-1, keepdims=True)
    h = (x * lax.rsqrt(ms + EPS) * nw_ref[...]).astype(BF16)

    def proj(idx):
        lo = _OFFS[idx]
        return jnp.dot(h, w_ref[:, lo:lo + _SIZES[idx]], preferred_element_type=F32)

    aq_ref[...] = (proj(0) * (QK_DIM ** -0.5 * LOG2E)).astype(BF16)
    ak_ref[...] = proj(1).astype(BF16)
    av_ref[0] = proj(2).T.astype(BF16)
    ga_ref[...] = _silu(proj(3)).astype(BF16)

    cos = cos_ref[...]
    sin = sin_ref[...]
    lane = lax.broadcasted_iota(jnp.int32, cos.shape, 1)
    first_half = (lane % QK_DIM) < (QK_DIM // 2)
    low_head = lane < QK_DIM

    def rotary_dup(t, scale):
        outs = []
        for half in range(2):
            th = t[:, half * LANES:(half + 1) * LANES]
            swapped = jnp.where(first_half, pltpu.roll(th, LANES - QK_DIM // 2, 1),
                                pltpu.roll(th, QK_DIM // 2, 1))
            r = (th * cos + swapped * sin) * scale
            rr = pltpu.roll(r, QK_DIM, 1)
            outs.append(jnp.where(low_head, r, rr))
            outs.append(jnp.where(low_head, rr, r))
        return jnp.concatenate(outs, axis=1).astype(BF16)

    bq_ref[...] = rotary_dup(proj(4), 1.0)
    bk_ref[...] = rotary_dup(proj(5), QK_DIM ** -0.5)
    bv_ref[...] = proj(6).astype(BF16)
    gb_ref[...] = _silu(proj(7)).astype(BF16)
    cu_ref[...] = proj(8)
    gc_ref[...] = _silu(proj(9)).astype(BF16)


def _proj_call(x2, norm_w, w_in, cos_t, sin_t, seq):
    rows = x2.shape[0]
    tm = PROJ_ROWS
    tiles_per_seq = seq // tm
    row_spec = lambda n: pl.BlockSpec((tm, n), lambda i: (i, 0))
    dtypes = [BF16] * 10
    dtypes[8] = F32
    out_shapes = [jax.ShapeDtypeStruct((rows, D_BRANCH), d) for d in dtypes]
    out_specs = [row_spec(D_BRANCH)] * 10
    out_shapes[2] = jax.ShapeDtypeStruct((rows // seq, D_BRANCH, seq), BF16)
    out_specs[2] = pl.BlockSpec((1, D_BRANCH, tm), lambda i: (i // tiles_per_seq, 0,
                                                              i % tiles_per_seq))
    return pl.pallas_call(
        _proj_kernel,
        grid=(rows // tm,),
        in_specs=[
            row_spec(D_MODEL),
            pl.BlockSpec((1, D_MODEL), lambda i: (0, 0)),
            pl.BlockSpec((D_MODEL, D_IN), lambda i: (0, 0), pipeline_mode=pl.Buffered(1)),
            pl.BlockSpec((tm, LANES), lambda i: (i % tiles_per_seq, 0)),
            pl.BlockSpec((tm, LANES), lambda i: (i % tiles_per_seq, 0)),
        ],
        out_specs=out_specs,
        out_shape=out_shapes,
        compiler_params=pltpu.CompilerParams(
            dimension_semantics=("arbitrary",), vmem_limit_bytes=VMEM_LIMIT),
        name="proj",
    )(x2, norm_w, w_in, cos_t, sin_t)


def _attn_step(p_now, p_prev, lam_ref, q_ref, k_ref, vt_ref, g_ref, bias_ref, sw_ref, o_ref,
               s_scr, m_scr, e_scr, c_scr, a_scr, *, seq, lam_init):
    tq = ATTN_Q
    nb = ATTN_K
    nkb = seq // BIAS_T
    nqb = tq // nb
    ncb = 2 * nqb
    vt = vt_ref[0]
    k_all = k_ref[0]
    lane = lax.broadcasted_iota(jnp.int32, (tq, LANES), 1)
    far_left = bias_ref[0, 3, 0:1, :]
    far_right = bias_ref[0, 4, 0:1, :]

    def key_groups(qb):
        lo, hi = max(qb - 1, 0), min(qb + 1, nkb - 1)
        return (slice(0, lo * BIAS_T), slice(lo * BIAS_T, (hi + 1) * BIAS_T),
                slice((hi + 1) * BIAS_T, seq)), range(lo, hi + 1)

    for sub in range(ATTN_SUB):
        rows = slice(sub * tq, (sub + 1) * tq)

        c_b = c_scr[sub, 0:1, :]
        o_t = jnp.concatenate(
            [jnp.dot(vt, e_scr[sub * ncb + n]
                     - c_b[:, n * nb:(n + 1) * nb] * e_scr[sub * ncb + nqb + n],
                     preferred_element_type=F32) for n in range(nqb)], axis=1)
        o = (o_t * a_scr[sub, 0:1, :]).T
        o = o * lax.rsqrt(jnp.mean(o * o, axis=-1, keepdims=True) + EPS) * sw_ref[...]
        o = o * (1.0 - lam_init)
        o_ref[0, rows, :] = (o * g_ref[0, rows, :].astype(F32)).astype(BF16)

        q = q_ref[0, rows, :]
        zero = jnp.zeros_like(q)
        q_st = jnp.concatenate([jnp.where(lane < QK_DIM, q, zero),
                                jnp.where(lane < QK_DIM, zero, q)], axis=0)
        m_rows = []
        l_new = []
        for n in range(ncb):
            cols = slice(n * nb, (n + 1) * nb)
            slab = sub * ncb + n
            groups, _ = key_groups((p_prev * ATTN_SUB + sub) * nqb + n % nqb)
            l_n = None
            for gi, grp in enumerate(groups):
                if grp.start == grp.stop:
                    continue
                e = jnp.exp2(s_scr[slab, grp, :] - m_scr[sub, gi:gi + 1, cols])
                e_scr[slab, grp, :] = e.astype(BF16)
                part = jnp.sum(e, axis=0, keepdims=True)
                l_n = part if l_n is None else l_n + part
            l_new.append(l_n)
            s = lax.dot_general(k_all, q_st[cols, :], (((1,), (1,)), ((), ())),
                                preferred_element_type=F32)
            qb = (p_now * ATTN_SUB + sub) * nqb + n % nqb
            (left, window, right), blocks = key_groups(qb)
            s_w = s[window, :] + jnp.concatenate([bias_ref[0, c - qb + 1] for c in blocks],
                                                 axis=0)
            s_scr[slab, window, :] = s_w
            m = jnp.max(s_w, axis=0, keepdims=True)
            if left.stop > left.start:
                s_scr[slab, left, :] = s[left, :]
                m = jnp.maximum(m, jnp.max(s[left, :], axis=0, keepdims=True) + far_left)
            if right.stop > right.start:
                s_scr[slab, right, :] = s[right, :]
                m = jnp.maximum(m, jnp.max(s[right, :], axis=0, keepdims=True) + far_right)
            m_rows.append(jnp.concatenate(
                [m - far_left, m, m - far_right, jnp.zeros((M_ROWS - 3, nb), F32)], axis=0))
        m_scr[sub] = jnp.concatenate(m_rows, axis=1)

        l_row = jnp.concatenate(l_new, axis=1)
        l1, l2 = l_row[:, :tq], l_row[:, tq:]
        c_scr[sub] = jnp.broadcast_to(lam_ref[0] * l1 / l2, c_scr.shape[1:]).astype(BF16)
        a_scr[sub] = jnp.broadcast_to(1.0 / l1, a_scr.shape[1:])


def _attn_kernel(lam_ref, q_ref, k_ref, vt_ref, g_ref, bias_ref, sw_ref, o_ref,
                 s_scr, m_scr, e_scr, c_scr, a_scr, *, seq, nq, lam_init):
    j = pl.program_id(0)

    @pl.when(j == 0)
    def _():
        s_scr[...] = jnp.zeros(s_scr.shape, F32)
        m_scr[...] = jnp.zeros(m_scr.shape, F32)
        e_scr[...] = jnp.zeros(e_scr.shape, BF16)
        c_scr[...] = jnp.zeros(c_scr.shape, BF16)
        a_scr[...] = jnp.zeros(a_scr.shape, F32)

    for p in range(nq):
        pl.when(j % nq == p)(functools.partial(
            _attn_step, p, (p - 1) % nq, lam_ref, q_ref, k_ref, vt_ref, g_ref, bias_ref, sw_ref,
            o_ref, s_scr, m_scr, e_scr, c_scr, a_scr, seq=seq, lam_init=lam_init))


def _attn_call(lam, aq, ak, avt, ga, bias_t, subln_w, lam_init):
    batch, seq, _ = aq.shape
    tq = ATTN_Q * ATTN_SUB
    nq = seq // tq
    n_tiles = batch * HEADS * nq
    n_slabs = ATTN_SUB * 2 * ATTN_Q // ATTN_K
    kern = functools.partial(_attn_kernel, seq=seq, nq=nq, lam_init=lam_init)

    def front(j):
        t = jnp.minimum(j, n_tiles - 1)
        return t // (HEADS * nq), (t // nq) % HEADS, t % nq

    def back(j):
        t = jnp.maximum(j - 2, 0)
        return t // (HEADS * nq), (t // nq) % HEADS, t % nq

    def tile_of(where):
        return lambda j: (where(j)[0], where(j)[2], where(j)[1])

    return pl.pallas_call(
        kern,
        grid=(n_tiles + 2,),
        in_specs=[
            pl.BlockSpec(memory_space=pltpu.SMEM),
            pl.BlockSpec((1, tq, LANES), tile_of(front)),
            pl.BlockSpec((1, seq, LANES), lambda j: (front(j)[0], 0, front(j)[1])),
            pl.BlockSpec((1, V_DIM, seq), lambda j: (back(j)[0], back(j)[1], 0)),
            pl.BlockSpec((1, tq, LANES), tile_of(back)),
            pl.BlockSpec((1,) + bias_t.shape[1:], lambda j: (front(j)[1], 0, 0, 0)),
            pl.BlockSpec((1, V_DIM), lambda j: (0, 0)),
        ],
        out_specs=pl.BlockSpec((1, tq, LANES), tile_of(back)),
        out_shape=jax.ShapeDtypeStruct((batch, seq, D_BRANCH), BF16),
        scratch_shapes=[
            pltpu.VMEM((n_slabs, seq, ATTN_K), F32),
            pltpu.VMEM((ATTN_SUB, M_ROWS, 2 * ATTN_Q), F32),
            pltpu.VMEM((n_slabs, seq, ATTN_K), BF16),
            pltpu.VMEM((ATTN_SUB, EDGE_ROWS, ATTN_Q), BF16),
            pltpu.VMEM((ATTN_SUB, M_ROWS, ATTN_Q), F32),
        ],
        compiler_params=pltpu.CompilerParams(
            dimension_semantics=("arbitrary",), vmem_limit_bytes=VMEM_LIMIT),
        name="diff_attn",
    )(lam, aq, ak, avt, ga, bias_t, subln_w)


def _ret_kernel(lg_ref, q_ref, k_ref, v_ref, g_ref, o_ref, kv_scr, r_scr, *, seq):
    C = RET_CHUNK
    nc = seq // C
    ri = lax.broadcasted_iota(jnp.int32, (C, LANES), 0).astype(F32)
    fwd_lane = lax.broadcasted_iota(jnp.int32, (C, LANES), 1) < QK_DIM
    dn = (lax.broadcasted_iota(jnp.int32, (C, C), 0)
          - lax.broadcasted_iota(jnp.int32, (C, C), 1)).astype(F32)
    top = lax.broadcasted_iota(jnp.int32, (LANES, V_DIM), 0) < QK_DIM

    for hh in range(RET_HEADS):
        hd = pl.program_id(1) * RET_HEADS + hh
        hl = slice(hh * LANES, (hh + 1) * LANES)
        lgf = lg_ref[0, hd]
        lgb = lg_ref[1, hd]
        q_scale = jnp.exp(jnp.where(fwd_lane, lgf * (ri + 1.0), lgb * (C - ri)))
        k_scale = jnp.exp(jnp.where(fwd_lane, lgf * (C - 1.0 - ri), lgb * ri))
        decay = 0.5 * jnp.where(dn >= 0.0, jnp.exp(lgf * jnp.maximum(dn, 0.0)),
                                jnp.exp(lgb * jnp.maximum(-dn, 0.0)))

        for c in range(nc):
            rows = slice(c * C, (c + 1) * C)
            kk = (k_ref[0, rows, hl].astype(F32) * k_scale).astype(BF16)
            kv_scr[hh * nc + c] = lax.dot_general(kk, v_ref[0, rows, hl], (((0,), (0,)), ((), ())),
                                                  preferred_element_type=F32)

        dec_f = jnp.exp(lgf * C)
        dec_b = jnp.exp(lgb * C)
        state = jnp.zeros((LANES, V_DIM), F32)
        for c in range(nc):
            r_scr[hh * nc + c] = state
            state = state * dec_f + kv_scr[hh * nc + c]
        state = jnp.zeros((LANES, V_DIM), F32)
        for c in reversed(range(nc)):
            r_scr[hh * nc + c] = jnp.where(top, r_scr[hh * nc + c], state)
            state = state * dec_b + kv_scr[hh * nc + c]

        for c in range(nc):
            rows = slice(c * C, (c + 1) * C)
            qc = q_ref[0, rows, hl]
            kc = k_ref[0, rows, hl]
            s = lax.dot_general(qc, kc, (((1,), (1,)), ((), ())), preferred_element_type=F32)
            a = (s * decay).astype(BF16)
            qq = (qc.astype(F32) * q_scale).astype(BF16)
            o = (jnp.dot(a, v_ref[0, rows, hl], preferred_element_type=F32)
                 + jnp.dot(qq, r_scr[hh * nc + c].astype(BF16), preferred_element_type=F32))
            o = o * lax.rsqrt(jnp.mean(o * o, axis=-1, keepdims=True) + EPS)
            o_ref[0, rows, hl] = (o * g_ref[0, rows, hl].astype(F32)).astype(BF16)


def _ret_call(log_gamma, bq, bk, bv, gb):
    batch, seq, _ = bq.shape
    nc = seq // RET_CHUNK
    spec = pl.BlockSpec((1, seq, RET_HEADS * LANES), lambda b, h: (b, 0, h))
    state_shape = (RET_HEADS * nc, LANES, V_DIM)
    return pl.pallas_call(
        functools.partial(_ret_kernel, seq=seq),
        grid=(batch, HEADS // RET_HEADS),
        in_specs=[pl.BlockSpec(memory_space=pltpu.SMEM), spec, spec, spec, spec],
        out_specs=spec,
        out_shape=jax.ShapeDtypeStruct((batch, seq, D_BRANCH), BF16),
        scratch_shapes=[pltpu.VMEM(state_shape, F32), pltpu.VMEM(state_shape, F32)],
        compiler_params=pltpu.CompilerParams(
            dimension_semantics=("arbitrary", "arbitrary"), vmem_limit_bytes=VMEM_LIMIT),
        name="retention",
    )(log_gamma, bq, bk, bv, gb)


def _out_kernel(oa_ref, ob_ref, cu_ref, cup_ref, cun_ref, gc_ref, x_ref, wo_ref, pw_ref, ps_ref,
                fnw_ref, o_ref, ext_scr, pooled_scr, *, seq, final):
    tm = cu_ref.shape[0]
    tiles_per_seq = seq // tm
    it = pl.program_id(0) % tiles_per_seq

    ext_scr[0:POOL_HALO, :] = jnp.where(it == 0, 0.0, cup_ref[...])
    ext_scr[POOL_HALO:POOL_HALO + tm, :] = cu_ref[...]
    ext_scr[POOL_HALO + tm:, :] = jnp.where(it == tiles_per_seq - 1, 0.0, cun_ref[...])

    n = tm + 2 * POOL_HALO
    body = slice(POOL_HALO, POOL_HALO + tm)
    for g, w in enumerate(POOL_WINDOWS):
        cols = slice(g * LANES, (g + 1) * LANES)
        run = {1: ext_scr[:, cols]}
        k = 1
        while 2 * k <= min(w, POOL_HALO):
            run[2 * k] = run[k] + pltpu.roll(run[k], n - k, 0)
            k *= 2
        if w == 2 * POOL_HALO:
            wsum = run[POOL_HALO][0:tm] + run[POOL_HALO][body]
        else:
            wsum = pltpu.roll(run[w], w // 2, 0)[body]
        u = cu_ref[:, cols]
        pooled_scr[:, cols] = (wsum * (1.0 / w) - u).astype(BF16)
        for r0 in (0, tm - EDGE_ROWS):
            edge = slice(r0, r0 + EDGE_ROWS)
            pos = it * tm + r0 + lax.broadcasted_iota(jnp.int32, (EDGE_ROWS, 1), 0)
            lo = jnp.maximum(pos - w // 2, 0)
            hi = jnp.minimum(pos + (w - w // 2), seq)
            count = (hi - lo).astype(F32)
            pooled_scr[edge, cols] = (wsum[edge] / count - u[edge]).astype(BF16)

    out = x_ref[...] + jnp.dot(oa_ref[...], wo_ref[0:D_BRANCH, :], preferred_element_type=F32)
    out = out + jnp.dot(ob_ref[...], wo_ref[D_BRANCH:2 * D_BRANCH, :],
                        preferred_element_type=F32)
    ys = [jnp.dot(pooled_scr[:, g * LANES:(g + 1) * LANES], pw_ref[g],
                  preferred_element_type=F32) for g in range(len(POOL_WINDOWS))]
    yc = jnp.concatenate(ys, axis=1) * ps_ref[...] * gc_ref[...].astype(F32)
    out = out + jnp.dot(yc.astype(BF16), wo_ref[2 * D_BRANCH:, :], preferred_element_type=F32)
    if final:
        out = out * lax.rsqrt(jnp.mean(out * out, axis=-1, keepdims=True) + EPS) * fnw_ref[...]
    o_ref[...] = out


def _out_call(oa, ob, cu, gc, x2, w_out, pool_w, pool_scale, final_norm_w, seq, final):
    rows = x2.shape[0]
    tm = OUT_ROWS
    halo_blocks = tm // POOL_HALO
    last_halo = rows // POOL_HALO - 1
    row_spec = lambda n: pl.BlockSpec((tm, n), lambda i: (i, 0))
    const = lambda shape: pl.BlockSpec(shape, lambda i: (0,) * len(shape))
    return pl.pallas_call(
        functools.partial(_out_kernel, seq=seq, final=final),
        grid=(rows // tm,),
        in_specs=[
            row_spec(D_BRANCH), row_spec(D_BRANCH), row_spec(D_BRANCH),
            pl.BlockSpec((POOL_HALO, D_BRANCH),
                         lambda i: (jnp.maximum(i * halo_blocks - 1, 0), 0)),
            pl.BlockSpec((POOL_HALO, D_BRANCH),
                         lambda i: (jnp.minimum((i + 1) * halo_blocks, last_halo), 0)),
            row_spec(D_BRANCH), row_spec(D_MODEL),
            const((3 * D_BRANCH, D_MODEL)),
            const((len(POOL_WINDOWS), LANES, LANES)),
            const((1, D_BRANCH)),
            const((1, D_MODEL)),
        ],
        out_specs=row_spec(D_MODEL),
        out_shape=jax.ShapeDtypeStruct((rows, D_MODEL), F32),
        scratch_shapes=[pltpu.VMEM((tm + 2 * POOL_HALO, D_BRANCH), F32),
                        pltpu.VMEM((tm, D_BRANCH), BF16)],
        compiler_params=pltpu.CompilerParams(
            dimension_semantics=("arbitrary",), vmem_limit_bytes=VMEM_LIMIT),
        name="pool_out",
    )(oa, ob, cu, cu, cu, gc, x2, w_out, pool_w, pool_scale, final_norm_w)


def _t5_bucket(rel):
    half = NUM_BUCKETS // 2
    max_exact = half // 2
    ret = jnp.where(rel > 0, half, 0)
    n = jnp.abs(rel)
    nf = jnp.maximum(n, 1).astype(F32)
    large = max_exact + (jnp.log(nf / max_exact) / math.log(MAX_DISTANCE / max_exact)
                         * (half - max_exact)).astype(jnp.int32)
    large = jnp.minimum(large, half - 1)
    return ret + jnp.where(n < max_exact, n, large)


def _bias_kernel(u_ref, o_ref):
    t = BIAS_T
    for r in range(u_ref.shape[1]):
        x = jnp.broadcast_to(u_ref[0, r:r + 1, :], (t, 2 * t))
        o_ref[0, r] = pltpu.roll(x, t + 1, 1, stride=1, stride_axis=0)[:, :t]


def _bias_tiles(rel_bias, seq):
    t = BIAS_T
    assert t >= MAX_DISTANCE and ATTN_Q % t == 0
    off = jnp.arange(2 * t, dtype=jnp.int32) - (t - 1)
    rel = jnp.stack([off - t, off, off + t, jnp.full_like(off, -seq), jnp.full_like(off, seq)])
    u = rel_bias.astype(F32)[_t5_bucket(rel)].transpose(2, 0, 1) * LOG2E
    u = jnp.roll(u[..., ::-1], -1, axis=-1)
    n_tiles = rel.shape[0]
    return pl.pallas_call(
        _bias_kernel,
        grid=(HEADS,),
        in_specs=[pl.BlockSpec((1, n_tiles, 2 * t), lambda h: (h, 0, 0))],
        out_specs=pl.BlockSpec((1, n_tiles, t, t), lambda h: (h, 0, 0, 0)),
        out_shape=jax.ShapeDtypeStruct((HEADS, n_tiles, t, t), F32),
        compiler_params=pltpu.CompilerParams(dimension_semantics=("arbitrary",)),
        name="bias_tiles",
    )(u)


def _rotary_tables(seq):
    half = QK_DIM // 2
    theta = 1.0 / (ROPE_BASE ** jnp.linspace(0.0, 1.0, half, dtype=F32))
    ang = jnp.arange(seq, dtype=F32)[:, None] * theta[None, :]
    cos = jnp.cos(ang)
    sin = jnp.sin(ang)
    reps = LANES // QK_DIM
    cos_t = jnp.tile(jnp.concatenate([cos, cos], axis=1), (1, reps))
    sin_t = jnp.tile(jnp.concatenate([-sin, sin], axis=1), (1, reps))
    return cos_t, sin_t


def kernel(x, norm_w, w_in, diff_lambda, diff_subln_w, ret_decay_logit, pool_w, pool_scale, w_out,
           rel_bias, final_norm_w):
    batch, seq, d_model = x.shape
    assert d_model == D_MODEL and seq % PROJ_ROWS == 0 and seq % OUT_ROWS == 0
    assert seq % (ATTN_Q * ATTN_SUB) == 0
    rows = batch * seq
    cos_t, sin_t = _rotary_tables(seq)
    bias_t = _bias_tiles(rel_bias, seq)
    fnw = final_norm_w.reshape(1, D_MODEL).astype(F32)

    h = x.reshape(rows, D_MODEL)
    for l in range(DEPTH):
        lam_init = 0.8 - 0.6 * math.exp(-0.3 * l)
        lf = diff_lambda[l].astype(F32)
        lam = (jnp.exp(jnp.sum(lf[0] * lf[1])) - jnp.exp(jnp.sum(lf[2] * lf[3])) + lam_init)
        log_gamma = jax.nn.log_sigmoid(ret_decay_logit[l].astype(F32))

        aq, ak, av, ga, bq, bk, bv, gb, cu, gc = _proj_call(
            h, norm_w[l].reshape(1, D_MODEL).astype(F32), w_in[l].astype(BF16), cos_t, sin_t, seq)
        to3 = lambda a: a.reshape(batch, seq, a.shape[-1])
        oa = _attn_call(lam.reshape(1), to3(aq), to3(ak), av, to3(ga), bias_t,
                        diff_subln_w[l].reshape(1, V_DIM).astype(F32), lam_init)
        ob = _ret_call(log_gamma, to3(bq), to3(bk), to3(bv), to3(gb))
        h = _out_call(oa.reshape(rows, D_BRANCH), ob.reshape(rows, D_BRANCH), cu, gc, h,
                      w_out[l].astype(BF16), pool_w[l].astype(BF16),
                      pool_scale[l].reshape(1, D_BRANCH).astype(F32), fnw, seq,
                      final=(l == DEPTH - 1))
    return h.reshape(batch, seq, D_MODEL)
```

```python
import functools
import math

import jax
import jax.numpy as jnp
from jax import lax
from jax.experimental import pallas as pl
from jax.experimental.pallas import tpu as pltpu

F32 = jnp.float32
BF16 = jnp.bfloat16

D_MODEL = 1024
DEPTH = 2
D_BRANCH = 512
HEADS = 4
QK_DIM = 64
V_DIM = 128
RET_CHUNK = 128
ROPE_BASE = 10000.0
POOL_WINDOWS = (2, 4, 8, 16)
POOL_HALO = 8
EDGE_ROWS = 16
NUM_BUCKETS = 32
MAX_DISTANCE = 128
EPS = 1e-6
LOG2E = math.log2(math.e)

_SIZES = (512, 512, 512, 512, 256, 256, 512, 512, 512, 512)
_OFFS = tuple(sum(_SIZES[:i]) for i in range(len(_SIZES)))
D_IN = sum(_SIZES)

LANES = 128
VMEM_LIMIT = 56 * 1024 * 1024

PROJ_ROWS = 1024
OUT_ROWS = 1024
RET_HEADS = 4
ATTN_Q = 512
ATTN_SUB = 2
M_ROWS = 8
ATTN_K = 256
BIAS_T = 256


def _silu(x):
    return x * jax.nn.sigmoid(x)


def _proj_kernel(x_ref, nw_ref, w_ref, cos_ref, sin_ref,
                 aq_ref, ak_ref, av_ref, ga_ref, bq_ref, bk_ref, bv_ref, gb_ref, cu_ref, gc_ref):
    x = x_ref[...]
    ms = jnp.mean(x * x, axis=-1, keepdims=True)
    h = (x * lax.rsqrt(ms + EPS) * nw_ref[...]).astype(BF16)

    def proj(idx):
        lo = _OFFS[idx]
        return jnp.dot(h, w_ref[:, lo:lo + _SIZES[idx]], preferred_element_type=F32)

    aq_ref[...] = (proj(0) * (QK_DIM ** -0.5 * LOG2E)).astype(BF16)
    ak_ref[...] = proj(1).astype(BF16)
    av_ref[0] = proj(2).T.astype(BF16)
    ga_ref[...] = _silu(proj(3)).astype(BF16)

    cos = cos_ref[...]
    sin = sin_ref[...]
    lane = lax.broadcasted_iota(jnp.int32, cos.shape, 1)
    first_half = (lane % QK_DIM) < (QK_DIM // 2)
    low_head = lane < QK_DIM

    def rotary_dup(t, scale):
        outs = []
        for half in range(2):
            th = t[:, half * LANES:(half + 1) * LANES]
            swapped = jnp.where(first_half, pltpu.roll(th, LANES - QK_DIM // 2, 1),
                                pltpu.roll(th, QK_DIM // 2, 1))
            r = (th * cos + swapped * sin) * scale
            rr = pltpu.roll(r, QK_DIM, 1)
            outs.append(jnp.where(low_head, r, rr))
            outs.append(jnp.where(low_head, rr, r))
        return jnp.concatenate(outs, axis=1).astype(BF16)

    bq_ref[...] = rotary_dup(proj(4), 1.0)
    bk_ref[...] = rotary_dup(proj(5), QK_DIM ** -0.5)
    bv_ref[...] = proj(6).astype(BF16)
    gb_ref[...] = _silu(proj(7)).astype(BF16)
    cu_ref[...] = proj(8)
    gc_ref[...] = _silu(proj(9)).astype(BF16)


def _proj_call(x2, norm_w, w_in, cos_t, sin_t, seq):
    rows = x2.shape[0]
    tm = PROJ_ROWS
    tiles_per_seq = seq // tm
    row_spec = lambda n: pl.BlockSpec((tm, n), lambda i: (i, 0))
    dtypes = [BF16] * 10
    dtypes[8] = F32
    out_shapes = [jax.ShapeDtypeStruct((rows, D_BRANCH), d) for d in dtypes]
    out_specs = [row_spec(D_BRANCH)] * 10
    out_shapes[2] = jax.ShapeDtypeStruct((rows // seq, D_BRANCH, seq), BF16)
    out_specs[2] = pl.BlockSpec((1, D_BRANCH, tm), lambda i: (i // tiles_per_seq, 0,
                                                              i % tiles_per_seq))
    return pl.pallas_call(
        _proj_kernel,
        grid=(rows // tm,),
        in_specs=[
            row_spec(D_MODEL),
            pl.BlockSpec((1, D_MODEL), lambda i: (0, 0)),
            pl.BlockSpec((D_MODEL, D_IN), lambda i: (0, 0), pipeline_mode=pl.Buffered(1)),
            pl.BlockSpec((tm, LANES), lambda i: (i % tiles_per_seq, 0)),
            pl.BlockSpec((tm, LANES), lambda i: (i % tiles_per_seq, 0)),
        ],
        out_specs=out_specs,
        out_shape=out_shapes,
        compiler_params=pltpu.CompilerParams(
            dimension_semantics=("arbitrary",), vmem_limit_bytes=VMEM_LIMIT),
        name="proj",
    )(x2, norm_w, w_in, cos_t, sin_t)


def _attn_step(p_now, p_prev, lam_ref, q_ref, k_ref, vt_ref, g_ref, bias_ref, sw_ref, o_ref,
               s_scr, m_scr, *, seq, lam_init):
    tq = ATTN_Q
    nb = ATTN_K
    nkb = seq // BIAS_T
    nqb = tq // nb
    ncb = 2 * nqb
    vt = jnp.concatenate([vt_ref[0], jnp.ones((EDGE_ROWS, seq), BF16)], axis=0)
    k_all = k_ref[0]
    lane = lax.broadcasted_iota(jnp.int32, (tq, LANES), 1)
    far_left = bias_ref[0, 3, 0:1, :]
    far_right = bias_ref[0, 4, 0:1, :]

    def key_groups(qb):
        lo, hi = max(qb - 1, 0), min(qb + 1, nkb - 1)
        return (slice(0, lo * BIAS_T), slice(lo * BIAS_T, (hi + 1) * BIAS_T),
                slice((hi + 1) * BIAS_T, seq)), range(lo, hi + 1)

    for sub in range(ATTN_SUB):
        rows = slice(sub * tq, (sub + 1) * tq)

        q = q_ref[0, rows, :]
        zero = jnp.zeros_like(q)
        q_st = jnp.concatenate([jnp.where(lane < QK_DIM, q, zero),
                                jnp.where(lane < QK_DIM, zero, q)], axis=0)
        m_rows = []
        r = []
        for n in range(ncb):
            cols = slice(n * nb, (n + 1) * nb)
            slab = sub * ncb + n
            groups, _ = key_groups((p_prev * ATTN_SUB + sub) * nqb + n % nqb)
            e = jnp.concatenate(
                [jnp.exp2(s_scr[slab, grp, :] - m_scr[sub, gi:gi + 1, cols]).astype(BF16)
                 for gi, grp in enumerate(groups) if grp.stop > grp.start], axis=0)
            r.append(jnp.dot(vt, e, preferred_element_type=F32))
            s = lax.dot_general(k_all, q_st[cols, :], (((1,), (1,)), ((), ())),
                                preferred_element_type=F32)
            qb = (p_now * ATTN_SUB + sub) * nqb + n % nqb
            (left, window, right), blocks = key_groups(qb)
            s_w = s[window, :] + jnp.concatenate([bias_ref[0, c - qb + 1] for c in blocks],
                                                 axis=0)
            s_scr[slab, window, :] = s_w
            m = jnp.max(s_w, axis=0, keepdims=True)
            if left.stop > left.start:
                s_scr[slab, left, :] = s[left, :]
                m = jnp.maximum(m, jnp.max(s[left, :], axis=0, keepdims=True) + far_left)
            if right.stop > right.start:
                s_scr[slab, right, :] = s[right, :]
                m = jnp.maximum(m, jnp.max(s[right, :], axis=0, keepdims=True) + far_right)
            m_rows.append(jnp.concatenate(
                [m - far_left, m, m - far_right, jnp.zeros((M_ROWS - 3, nb), F32)], axis=0))
        m_scr[sub] = jnp.concatenate(m_rows, axis=1)

        r1 = jnp.concatenate(r[:nqb], axis=1)
        r2 = jnp.concatenate(r[nqb:], axis=1)
        o_t = (r1[:V_DIM] * (1.0 / r1[V_DIM:V_DIM + 1])
               - r2[:V_DIM] * (lam_ref[0] / r2[V_DIM:V_DIM + 1]))
        o = o_t.T
        o = o * lax.rsqrt(jnp.mean(o * o, axis=-1, keepdims=True) + EPS) * sw_ref[...]
        o = o * (1.0 - lam_init)
        o_ref[0, rows, :] = (o * g_ref[0, rows, :].astype(F32)).astype(BF16)


def _attn_kernel(lam_ref, q_ref, k_ref, vt_ref, g_ref, bias_ref, sw_ref, o_ref,
                 s_scr, m_scr, *, seq, nq, lam_init):
    j = pl.program_id(0)

    @pl.when(j == 0)
    def _():
        s_scr[...] = jnp.zeros(s_scr.shape, F32)
        m_scr[...] = jnp.zeros(m_scr.shape, F32)

    for p in range(nq):
        pl.when(j % nq == p)(functools.partial(
            _attn_step, p, (p - 1) % nq, lam_ref, q_ref, k_ref, vt_ref, g_ref, bias_ref, sw_ref,
            o_ref, s_scr, m_scr, seq=seq, lam_init=lam_init))


def _attn_call(lam, aq, ak, avt, ga, bias_t, subln_w, lam_init):
    batch, seq, _ = aq.shape
    tq = ATTN_Q * ATTN_SUB
    nq = seq // tq
    n_tiles = batch * HEADS * nq
    n_slabs = ATTN_SUB * 2 * ATTN_Q // ATTN_K
    kern = functools.partial(_attn_kernel, seq=seq, nq=nq, lam_init=lam_init)

    def front(j):
        t = jnp.minimum(j, n_tiles - 1)
        return t // (HEADS * nq), (t // nq) % HEADS, t % nq

    def back(j):
        t = jnp.maximum(j - 1, 0)
        return t // (HEADS * nq), (t // nq) % HEADS, t % nq

    def tile_of(where):
        return lambda j: (where(j)[0], where(j)[2], where(j)[1])

    return pl.pallas_call(
        kern,
        grid=(n_tiles + 1,),
        in_specs=[
            pl.BlockSpec(memory_space=pltpu.SMEM),
            pl.BlockSpec((1, tq, LANES), tile_of(front)),
            pl.BlockSpec((1, seq, LANES), lambda j: (front(j)[0], 0, front(j)[1])),
            pl.BlockSpec((1, V_DIM, seq), lambda j: (back(j)[0], back(j)[1], 0)),
            pl.BlockSpec((1, tq, LANES), tile_of(back)),
            pl.BlockSpec((1,) + bias_t.shape[1:], lambda j: (front(j)[1], 0, 0, 0)),
            pl.BlockSpec((1, V_DIM), lambda j: (0, 0)),
        ],
        out_specs=pl.BlockSpec((1, tq, LANES), tile_of(back)),
        out_shape=jax.ShapeDtypeStruct((batch, seq, D_BRANCH), BF16),
        scratch_shapes=[
            pltpu.VMEM((n_slabs, seq, ATTN_K), F32),
            pltpu.VMEM((ATTN_SUB, M_ROWS, 2 * ATTN_Q), F32),
        ],
        compiler_params=pltpu.CompilerParams(
            dimension_semantics=("arbitrary",), vmem_limit_bytes=VMEM_LIMIT),
        name="diff_attn",
    )(lam, aq, ak, avt, ga, bias_t, subln_w)


def _ret_kernel(lg_ref, q_ref, k_ref, v_ref, g_ref, o_ref, kv_scr, r_scr, *, seq):
    C = RET_CHUNK
    nc = seq // C
    ri = lax.broadcasted_iota(jnp.int32, (C, LANES), 0).astype(F32)
    fwd_lane = lax.broadcasted_iota(jnp.int32, (C, LANES), 1) < QK_DIM
    dn = (lax.broadcasted_iota(jnp.int32, (C, C), 0)
          - lax.broadcasted_iota(jnp.int32, (C, C), 1)).astype(F32)
    top = lax.broadcasted_iota(jnp.int32, (LANES, V_DIM), 0) < QK_DIM

    for hh in range(RET_HEADS):
        hd = pl.program_id(1) * RET_HEADS + hh
        hl = slice(hh * LANES, (hh + 1) * LANES)
        lgf = lg_ref[0, hd]
        lgb = lg_ref[1, hd]
        q_scale = jnp.exp(jnp.where(fwd_lane, lgf * (ri + 1.0), lgb * (C - ri)))
        k_scale = jnp.exp(jnp.where(fwd_lane, lgf * (C - 1.0 - ri), lgb * ri))
        decay = 0.5 * jnp.where(dn >= 0.0, jnp.exp(lgf * jnp.maximum(dn, 0.0)),
                                jnp.exp(lgb * jnp.maximum(-dn, 0.0)))

        for c in range(nc):
            rows = slice(c * C, (c + 1) * C)
            kk = (k_ref[0, rows, hl].astype(F32) * k_scale).astype(BF16)
            kv_scr[hh * nc + c] = lax.dot_general(kk, v_ref[0, rows, hl], (((0,), (0,)), ((), ())),
                                                  preferred_element_type=F32)

        dec_f = jnp.exp(lgf * C)
        dec_b = jnp.exp(lgb * C)
        state = jnp.zeros((LANES, V_DIM), F32)
        for c in range(nc):
            r_scr[hh * nc + c] = state
            state = state * dec_f + kv_scr[hh * nc + c]
        state = jnp.zeros((LANES, V_DIM), F32)
        for c in reversed(range(nc)):
            r_scr[hh * nc + c] = jnp.where(top, r_scr[hh * nc + c], state)
            state = state * dec_b + kv_scr[hh * nc + c]

        for c in range(nc):
            rows = slice(c * C, (c + 1) * C)
            qc = q_ref[0, rows, hl]
            kc = k_ref[0, rows, hl]
            s = lax.dot_general(qc, kc, (((1,), (1,)), ((), ())), preferred_element_type=F32)
            a = (s * decay).astype(BF16)
            qq = (qc.astype(F32) * q_scale).astype(BF16)
            o = (jnp.dot(a, v_ref[0, rows, hl], preferred_element_type=F32)
                 + jnp.dot(qq, r_scr[hh * nc + c].astype(BF16), preferred_element_type=F32))
            o = o * lax.rsqrt(jnp.mean(o * o, axis=-1, keepdims=True) + EPS)
            o_ref[0, rows, hl] = (o * g_ref[0, rows, hl].astype(F32)).astype(BF16)


def _ret_call(log_gamma, bq, bk, bv, gb):
    batch, seq, _ = bq.shape
    nc = seq // RET_CHUNK
    spec = pl.BlockSpec((1, seq, RET_HEADS * LANES), lambda b, h: (b, 0, h))
    state_shape = (RET_HEADS * nc, LANES, V_DIM)
    return pl.pallas_call(
        functools.partial(_ret_kernel, seq=seq),
        grid=(batch, HEADS // RET_HEADS),
        in_specs=[pl.BlockSpec(memory_space=pltpu.SMEM), spec, spec, spec, spec],
        out_specs=spec,
        out_shape=jax.ShapeDtypeStruct((batch, seq, D_BRANCH), BF16),
        scratch_shapes=[pltpu.VMEM(state_shape, F32), pltpu.VMEM(state_shape, F32)],
        compiler_params=pltpu.CompilerParams(
            dimension_semantics=("arbitrary", "arbitrary"), vmem_limit_bytes=VMEM_LIMIT),
        name="retention",
    )(log_gamma, bq, bk, bv, gb)


def _out_kernel(oa_ref, ob_ref, cu_ref, cup_ref, cun_ref, gc_ref, x_ref, wo_ref, pw_ref, ps_ref,
                fnw_ref, o_ref, ext_scr, pooled_scr, *, seq, final):
    tm = cu_ref.shape[0]
    tiles_per_seq = seq // tm
    it = pl.program_id(0) % tiles_per_seq

    ext_scr[0:POOL_HALO, :] = jnp.where(it == 0, 0.0, cup_ref[...])
    ext_scr[POOL_HALO:POOL_HALO + tm, :] = cu_ref[...]
    ext_scr[POOL_HALO + tm:, :] = jnp.where(it == tiles_per_seq - 1, 0.0, cun_ref[...])

    n = tm + 2 * POOL_HALO
    body = slice(POOL_HALO, POOL_HALO + tm)
    for g, w in enumerate(POOL_WINDOWS):
        cols = slice(g * LANES, (g + 1) * LANES)
        run = {1: ext_scr[:, cols]}
        k = 1
        while 2 * k <= min(w, POOL_HALO):
            run[2 * k] = run[k] + pltpu.roll(run[k], n - k, 0)
            k *= 2
        if w == 2 * POOL_HALO:
            wsum = run[POOL_HALO][0:tm] + run[POOL_HALO][body]
        else:
            wsum = pltpu.roll(run[w], w // 2, 0)[body]
        u = cu_ref[:, cols]
        pooled_scr[:, cols] = (wsum * (1.0 / w) - u).astype(BF16)
        for r0 in (0, tm - EDGE_ROWS):
            edge = slice(r0, r0 + EDGE_ROWS)
            pos = it * tm + r0 + lax.broadcasted_iota(jnp.int32, (EDGE_ROWS, 1), 0)
            lo = jnp.maximum(pos - w // 2, 0)
            hi = jnp.minimum(pos + (w - w // 2), seq)
            count = (hi - lo).astype(F32)
            pooled_scr[edge, cols] = (wsum[edge] / count - u[edge]).astype(BF16)

    out = x_ref[...] + jnp.dot(oa_ref[...], wo_ref[0:D_BRANCH, :], preferred_element_type=F32)
    out = out + jnp.dot(ob_ref[...], wo_ref[D_BRANCH:2 * D_BRANCH, :],
                        preferred_element_type=F32)
    ys = [jnp.dot(pooled_scr[:, g * LANES:(g + 1) * LANES], pw_ref[g],
                  preferred_element_type=F32) for g in range(len(POOL_WINDOWS))]
    yc = jnp.concatenate(ys, axis=1) * ps_ref[...] * gc_ref[...].astype(F32)
    out = out + jnp.dot(yc.astype(BF16), wo_ref[2 * D_BRANCH:, :], preferred_element_type=F32)
    if final:
        out = out * lax.rsqrt(jnp.mean(out * out, axis=-1, keepdims=True) + EPS) * fnw_ref[...]
    o_ref[...] = out


def _out_call(oa, ob, cu, gc, x2, w_out, pool_w, pool_scale, final_norm_w, seq, final):
    rows = x2.shape[0]
    tm = OUT_ROWS
    halo_blocks = tm // POOL_HALO
    last_halo = rows // POOL_HALO - 1
    row_spec = lambda n: pl.BlockSpec((tm, n), lambda i: (i, 0))
    const = lambda shape: pl.BlockSpec(shape, lambda i: (0,) * len(shape))
    return pl.pallas_call(
        functools.partial(_out_kernel, seq=seq, final=final),
        grid=(rows // tm,),
        in_specs=[
            row_spec(D_BRANCH), row_spec(D_BRANCH), row_spec(D_BRANCH),
            pl.BlockSpec((POOL_HALO, D_BRANCH),
                         lambda i: (jnp.maximum(i * halo_blocks - 1, 0), 0)),
            pl.BlockSpec((POOL_HALO, D_BRANCH),
                         lambda i: (jnp.minimum((i + 1) * halo_blocks, last_halo), 0)),
            row_spec(D_BRANCH), row_spec(D_MODEL),
            const((3 * D_BRANCH, D_MODEL)),
            const((len(POOL_WINDOWS), LANES, LANES)),
            const((1, D_BRANCH)),
            const((1, D_MODEL)),
        ],
        out_specs=row_spec(D_MODEL),
        out_shape=jax.ShapeDtypeStruct((rows, D_MODEL), F32),
        scratch_shapes=[pltpu.VMEM((tm + 2 * POOL_HALO, D_BRANCH), F32),
                        pltpu.VMEM((tm, D_BRANCH), BF16)],
        compiler_params=pltpu.CompilerParams(
            dimension_semantics=("arbitrary",), vmem_limit_bytes=VMEM_LIMIT),
        name="pool_out",
    )(oa, ob, cu, cu, cu, gc, x2, w_out, pool_w, pool_scale, final_norm_w)


def _t5_bucket(rel):
    half = NUM_BUCKETS // 2
    max_exact = half // 2
    ret = jnp.where(rel > 0, half, 0)
    n = jnp.abs(rel)
    nf = jnp.maximum(n, 1).astype(F32)
    large = max_exact + (jnp.log(nf / max_exact) / math.log(MAX_DISTANCE / max_exact)
                         * (half - max_exact)).astype(jnp.int32)
    large = jnp.minimum(large, half - 1)
    return ret + jnp.where(n < max_exact, n, large)


def _bias_kernel(u_ref, o_ref):
    t = BIAS_T
    for r in range(u_ref.shape[1]):
        x = jnp.broadcast_to(u_ref[0, r:r + 1, :], (t, 2 * t))
        o_ref[0, r] = pltpu.roll(x, t + 1, 1, stride=1, stride_axis=0)[:, :t]


def _bias_tiles(rel_bias, seq):
    t = BIAS_T
    assert t >= MAX_DISTANCE and ATTN_Q % t == 0
    off = jnp.arange(2 * t, dtype=jnp.int32) - (t - 1)
    rel = jnp.stack([off - t, off, off + t, jnp.full_like(off, -seq), jnp.full_like(off, seq)])
    u = rel_bias.astype(F32)[_t5_bucket(rel)].transpose(2, 0, 1) * LOG2E
    u = jnp.roll(u[..., ::-1], -1, axis=-1)
    n_tiles = rel.shape[0]
    return pl.pallas_call(
        _bias_kernel,
        grid=(HEADS,),
        in_specs=[pl.BlockSpec((1, n_tiles, 2 * t), lambda h: (h, 0, 0))],
        out_specs=pl.BlockSpec((1, n_tiles, t, t), lambda h: (h, 0, 0, 0)),
        out_shape=jax.ShapeDtypeStruct((HEADS, n_tiles, t, t), F32),
        compiler_params=pltpu.CompilerParams(dimension_semantics=("arbitrary",)),
        name="bias_tiles",
    )(u)


def _rotary_tables(seq):
    half = QK_DIM // 2
    theta = 1.0 / (ROPE_BASE ** jnp.linspace(0.0, 1.0, half, dtype=F32))
    ang = jnp.arange(seq, dtype=F32)[:, None] * theta[None, :]
    cos = jnp.cos(ang)
    sin = jnp.sin(ang)
    reps = LANES // QK_DIM
    cos_t = jnp.tile(jnp.concatenate([cos, cos], axis=1), (1, reps))
    sin_t = jnp.tile(jnp.concatenate([-sin, sin], axis=1), (1, reps))
    return cos_t, sin_t


def kernel(x, norm_w, w_in, diff_lambda, diff_subln_w, ret_decay_logit, pool_w, pool_scale, w_out,
           rel_bias, final_norm_w):
    batch, seq, d_model = x.shape
    assert d_model == D_MODEL and seq % PROJ_ROWS == 0 and seq % OUT_ROWS == 0
    assert seq % (ATTN_Q * ATTN_SUB) == 0
    rows = batch * seq
    cos_t, sin_t = _rotary_tables(seq)
    bias_t = _bias_tiles(rel_bias, seq)
    fnw = final_norm_w.reshape(1, D_MODEL).astype(F32)

    h = x.reshape(rows, D_MODEL)
    for l in range(DEPTH):
        lam_init = 0.8 - 0.6 * math.exp(-0.3 * l)
        lf = diff_lambda[l].astype(F32)
        lam = (jnp.exp(jnp.sum(lf[0] * lf[1])) - jnp.exp(jnp.sum(lf[2] * lf[3])) + lam_init)
        log_gamma = jax.nn.log_sigmoid(ret_decay_logit[l].astype(F32))

        aq, ak, av, ga, bq, bk, bv, gb, cu, gc = _proj_call(
            h, norm_w[l].reshape(1, D_MODEL).astype(F32), w_in[l].astype(BF16), cos_t, sin_t, seq)
        to3 = lambda a: a.reshape(batch, seq, a.shape[-1])
        oa = _attn_call(lam.reshape(1), to3(aq), to3(ak), av, to3(ga), bias_t,
                        diff_subln_w[l].reshape(1, V_DIM).astype(F32), lam_init)
        ob = _ret_call(log_gamma, to3(bq), to3(bk), to3(bv), to3(gb))
        h = _out_call(oa.reshape(rows, D_BRANCH), ob.reshape(rows, D_BRANCH), cu, gc, h,
                      w_out[l].astype(BF16), pool_w[l].astype(BF16),
                      pool_scale[l].reshape(1, D_BRANCH).astype(F32), fnw, seq,
                      final=(l == DEPTH - 1))
    return h.reshape(batch, seq, D_MODEL)
```

```python
import functools
import math

import jax
import jax.numpy as jnp
from jax import lax
from jax.experimental import pallas as pl
from jax.experimental.pallas import tpu as pltpu

F32 = jnp.float32
BF16 = jnp.bfloat16

D_MODEL = 1024
DEPTH = 2
D_BRANCH = 512
HEADS = 4
QK_DIM = 64
V_DIM = 128
RET_CHUNK = 128
ROPE_BASE = 10000.0
POOL_WINDOWS = (2, 4, 8, 16)
POOL_HALO = 8
EDGE_ROWS = 16
NUM_BUCKETS = 32
MAX_DISTANCE = 128
EPS = 1e-6
LOG2E = math.log2(math.e)

_SIZES = (512, 512, 512, 512, 256, 256, 512, 512, 512, 512)
_OFFS = tuple(sum(_SIZES[:i]) for i in range(len(_SIZES)))
D_IN = sum(_SIZES)

LANES = 128
VMEM_LIMIT = 56 * 1024 * 1024

PROJ_ROWS = 1024
OUT_ROWS = 1024
RET_HEADS = 4
ATTN_Q = 256
ATTN_SUB = 4
M_ROWS = 8
ATTN_K = 256
BIAS_T = 256


def _silu(x):
    return x * jax.nn.sigmoid(x)


def _proj_kernel(x_ref, nw_ref, w_ref, cos_ref, sin_ref,
                 aq_ref, ak_ref, av_ref, ga_ref, bq_ref, bk_ref, bv_ref, gb_ref, cu_ref, gc_ref):
    x = x_ref[...]
    ms = jnp.mean(x * x, axis=-1, keepdims=True)
    h = (x * lax.rsqrt(ms + EPS) * nw_ref[...]).astype(BF16)

    def proj(idx):
        lo = _OFFS[idx]
        return jnp.dot(h, w_ref[:, lo:lo + _SIZES[idx]], preferred_element_type=F32)

    aq_ref[...] = (proj(0) * (QK_DIM ** -0.5 * LOG2E)).astype(BF16)
    ak_ref[...] = proj(1).astype(BF16)
    av_ref[0] = proj(2).T.astype(BF16)
    ga_ref[...] = _silu(proj(3)).astype(BF16)

    cos = cos_ref[...]
    sin = sin_ref[...]
    lane = lax.broadcasted_iota(jnp.int32, cos.shape, 1)
    first_half = (lane % QK_DIM) < (QK_DIM // 2)
    low_head = lane < QK_DIM

    def rotary_dup(t, scale):
        outs = []
        for half in range(2):
            th = t[:, half * LANES:(half + 1) * LANES]
            swapped = jnp.where(first_half, pltpu.roll(th, LANES - QK_DIM // 2, 1),
                                pltpu.roll(th, QK_DIM // 2, 1))
            r = (th * cos + swapped * sin) * scale
            rr = pltpu.roll(r, QK_DIM, 1)
            outs.append(jnp.where(low_head, r, rr))
            outs.append(jnp.where(low_head, rr, r))
        return jnp.concatenate(outs, axis=1).astype(BF16)

    bq_ref[...] = rotary_dup(proj(4), 1.0)
    bk_ref[...] = rotary_dup(proj(5), QK_DIM ** -0.5)
    bv_ref[...] = proj(6).astype(BF16)
    gb_ref[...] = _silu(proj(7)).astype(BF16)
    cu_ref[...] = proj(8)
    gc_ref[...] = _silu(proj(9)).astype(BF16)


def _proj_call(x2, norm_w, w_in, cos_t, sin_t, seq):
    rows = x2.shape[0]
    tm = PROJ_ROWS
    tiles_per_seq = seq // tm
    row_spec = lambda n: pl.BlockSpec((tm, n), lambda i: (i, 0))
    dtypes = [BF16] * 10
    dtypes[8] = F32
    out_shapes = [jax.ShapeDtypeStruct((rows, D_BRANCH), d) for d in dtypes]
    out_specs = [row_spec(D_BRANCH)] * 10
    out_shapes[2] = jax.ShapeDtypeStruct((rows // seq, D_BRANCH, seq), BF16)
    out_specs[2] = pl.BlockSpec((1, D_BRANCH, tm), lambda i: (i // tiles_per_seq, 0,
                                                              i % tiles_per_seq))
    return pl.pallas_call(
        _proj_kernel,
        grid=(rows // tm,),
        in_specs=[
            row_spec(D_MODEL),
            pl.BlockSpec((1, D_MODEL), lambda i: (0, 0)),
            pl.BlockSpec((D_MODEL, D_IN), lambda i: (0, 0), pipeline_mode=pl.Buffered(1)),
            pl.BlockSpec((tm, LANES), lambda i: (i % tiles_per_seq, 0)),
            pl.BlockSpec((tm, LANES), lambda i: (i % tiles_per_seq, 0)),
        ],
        out_specs=out_specs,
        out_shape=out_shapes,
        compiler_params=pltpu.CompilerParams(
            dimension_semantics=("arbitrary",), vmem_limit_bytes=VMEM_LIMIT),
        name="proj",
    )(x2, norm_w, w_in, cos_t, sin_t)


def _attn_step(p_now, p_prev, lam_ref, q_ref, k_ref, vt_ref, g_ref, bias_ref, sw_ref, o_ref,
               s_scr, m_scr, e_scr, *, seq, lam_init):
    tq = ATTN_Q
    nb = ATTN_K
    nkb = seq // BIAS_T
    nqb = tq // nb
    ncb = 2 * nqb
    vt = jnp.concatenate([vt_ref[0], jnp.ones((EDGE_ROWS, seq), BF16)], axis=0)
    k_all = k_ref[0]
    lane = lax.broadcasted_iota(jnp.int32, (tq, LANES), 1)
    far_left = bias_ref[0, 3, 0:1, :]
    far_right = bias_ref[0, 4, 0:1, :]

    def key_groups(qb):
        lo, hi = max(qb - 1, 0), min(qb + 1, nkb - 1)
        return (slice(0, lo * BIAS_T), slice(lo * BIAS_T, (hi + 1) * BIAS_T),
                slice((hi + 1) * BIAS_T, seq)), range(lo, hi + 1)

    for sub in range(ATTN_SUB):
        rows = slice(sub * tq, (sub + 1) * tq)

        r = [jnp.dot(vt, e_scr[(p_now + 1) % 2, sub * ncb + n], preferred_element_type=F32)
             for n in range(ncb)]
        r1 = jnp.concatenate(r[:nqb], axis=1)
        r2 = jnp.concatenate(r[nqb:], axis=1)
        o_t = (r1[:V_DIM] * (1.0 / r1[V_DIM:V_DIM + 1])
               - r2[:V_DIM] * (lam_ref[0] / r2[V_DIM:V_DIM + 1]))
        o = o_t.T
        o = o * lax.rsqrt(jnp.mean(o * o, axis=-1, keepdims=True) + EPS) * sw_ref[...]
        o = o * (1.0 - lam_init)
        o_ref[0, rows, :] = (o * g_ref[0, rows, :].astype(F32)).astype(BF16)

        q = q_ref[0, rows, :]
        zero = jnp.zeros_like(q)
        q_st = jnp.concatenate([jnp.where(lane < QK_DIM, q, zero),
                                jnp.where(lane < QK_DIM, zero, q)], axis=0)
        m_rows = []
        for n in range(ncb):
            cols = slice(n * nb, (n + 1) * nb)
            slab = sub * ncb + n
            groups, _ = key_groups((p_prev * ATTN_SUB + sub) * nqb + n % nqb)
            for gi, grp in enumerate(groups):
                if grp.start == grp.stop:
                    continue
                e_scr[p_now % 2, slab, grp, :] = jnp.exp2(
                    s_scr[slab, grp, :] - m_scr[sub, gi:gi + 1, cols]).astype(BF16)
            s = lax.dot_general(k_all, q_st[cols, :], (((1,), (1,)), ((), ())),
                                preferred_element_type=F32)
            qb = (p_now * ATTN_SUB + sub) * nqb + n % nqb
            (left, window, right), blocks = key_groups(qb)
            s_w = s[window, :] + jnp.concatenate([bias_ref[0, c - qb + 1] for c in blocks],
                                                 axis=0)
            s_scr[slab, window, :] = s_w
            m = jnp.max(s_w, axis=0, keepdims=True)
            if left.stop > left.start:
                s_scr[slab, left, :] = s[left, :]
                m = jnp.maximum(m, jnp.max(s[left, :], axis=0, keepdims=True) + far_left)
            if right.stop > right.start:
                s_scr[slab, right, :] = s[right, :]
                m = jnp.maximum(m, jnp.max(s[right, :], axis=0, keepdims=True) + far_right)
            m_rows.append(jnp.concatenate(
                [m - far_left, m, m - far_right, jnp.zeros((M_ROWS - 3, nb), F32)], axis=0))
        m_scr[sub] = jnp.concatenate(m_rows, axis=1)


def _attn_kernel(lam_ref, q_ref, k_ref, vt_ref, g_ref, bias_ref, sw_ref, o_ref,
                 s_scr, m_scr, e_scr, *, seq, nq, lam_init):
    j = pl.program_id(0)

    @pl.when(j == 0)
    def _():
        s_scr[...] = jnp.zeros(s_scr.shape, F32)
        m_scr[...] = jnp.zeros(m_scr.shape, F32)
        e_scr[...] = jnp.ones(e_scr.shape, BF16)

    for p in range(nq):
        pl.when(j % nq == p)(functools.partial(
            _attn_step, p, (p - 1) % nq, lam_ref, q_ref, k_ref, vt_ref, g_ref, bias_ref, sw_ref,
            o_ref, s_scr, m_scr, e_scr, seq=seq, lam_init=lam_init))


def _attn_call(lam, aq, ak, avt, ga, bias_t, subln_w, lam_init):
    batch, seq, _ = aq.shape
    tq = ATTN_Q * ATTN_SUB
    nq = seq // tq
    n_tiles = batch * HEADS * nq
    n_slabs = ATTN_SUB * 2 * ATTN_Q // ATTN_K
    assert nq % 2 == 0
    kern = functools.partial(_attn_kernel, seq=seq, nq=nq, lam_init=lam_init)

    def front(j):
        t = jnp.minimum(j, n_tiles - 1)
        return t // (HEADS * nq), (t // nq) % HEADS, t % nq

    def back(j):
        t = jnp.maximum(j - 2, 0)
        return t // (HEADS * nq), (t // nq) % HEADS, t % nq

    def tile_of(where):
        return lambda j: (where(j)[0], where(j)[2], where(j)[1])

    return pl.pallas_call(
        kern,
        grid=(n_tiles + 2,),
        in_specs=[
            pl.BlockSpec(memory_space=pltpu.SMEM),
            pl.BlockSpec((1, tq, LANES), tile_of(front)),
            pl.BlockSpec((1, seq, LANES), lambda j: (front(j)[0], 0, front(j)[1])),
            pl.BlockSpec((1, V_DIM, seq), lambda j: (back(j)[0], back(j)[1], 0)),
            pl.BlockSpec((1, tq, LANES), tile_of(back)),
            pl.BlockSpec((1,) + bias_t.shape[1:], lambda j: (front(j)[1], 0, 0, 0)),
            pl.BlockSpec((1, V_DIM), lambda j: (0, 0)),
        ],
        out_specs=pl.BlockSpec((1, tq, LANES), tile_of(back)),
        out_shape=jax.ShapeDtypeStruct((batch, seq, D_BRANCH), BF16),
        scratch_shapes=[
            pltpu.VMEM((n_slabs, seq, ATTN_K), F32),
            pltpu.VMEM((ATTN_SUB, M_ROWS, 2 * ATTN_Q), F32),
            pltpu.VMEM((2, n_slabs, seq, ATTN_K), BF16),
        ],
        compiler_params=pltpu.CompilerParams(
            dimension_semantics=("arbitrary",), vmem_limit_bytes=VMEM_LIMIT),
        name="diff_attn",
    )(lam, aq, ak, avt, ga, bias_t, subln_w)


def _ret_kernel(lg_ref, q_ref, k_ref, v_ref, g_ref, o_ref, kv_scr, r_scr, *, seq):
    C = RET_CHUNK
    nc = seq // C
    ri = lax.broadcasted_iota(jnp.int32, (C, LANES), 0).astype(F32)
    fwd_lane = lax.broadcasted_iota(jnp.int32, (C, LANES), 1) < QK_DIM
    dn = (lax.broadcasted_iota(jnp.int32, (C, C), 0)
          - lax.broadcasted_iota(jnp.int32, (C, C), 1)).astype(F32)
    top = lax.broadcasted_iota(jnp.int32, (LANES, V_DIM), 0) < QK_DIM

    for hh in range(RET_HEADS):
        hd = pl.program_id(1) * RET_HEADS + hh
        hl = slice(hh * LANES, (hh + 1) * LANES)
        lgf = lg_ref[0, hd]
        lgb = lg_ref[1, hd]
        q_scale = jnp.exp(jnp.where(fwd_lane, lgf * (ri + 1.0), lgb * (C - ri)))
        k_scale = jnp.exp(jnp.where(fwd_lane, lgf * (C - 1.0 - ri), lgb * ri))
        decay = 0.5 * jnp.where(dn >= 0.0, jnp.exp(lgf * jnp.maximum(dn, 0.0)),
                                jnp.exp(lgb * jnp.maximum(-dn, 0.0)))

        for c in range(nc):
            rows = slice(c * C, (c + 1) * C)
            kk = (k_ref[0, rows, hl].astype(F32) * k_scale).astype(BF16)
            kv_scr[hh * nc + c] = lax.dot_general(kk, v_ref[0, rows, hl], (((0,), (0,)), ((), ())),
                                                  preferred_element_type=F32)

        dec_f = jnp.exp(lgf * C)
        dec_b = jnp.exp(lgb * C)
        state = jnp.zeros((LANES, V_DIM), F32)
        for c in range(nc):
            r_scr[hh * nc + c] = state
            state = state * dec_f + kv_scr[hh * nc + c]
        state = jnp.zeros((LANES, V_DIM), F32)
        for c in reversed(range(nc)):
            r_scr[hh * nc + c] = jnp.where(top, r_scr[hh * nc + c], state)
            state = state * dec_b + kv_scr[hh * nc + c]

        for c in range(nc):
            rows = slice(c * C, (c + 1) * C)
            qc = q_ref[0, rows, hl]
            kc = k_ref[0, rows, hl]
            s = lax.dot_general(qc, kc, (((1,), (1,)), ((), ())), preferred_element_type=F32)
            a = (s * decay).astype(BF16)
            qq = (qc.astype(F32) * q_scale).astype(BF16)
            o = (jnp.dot(a, v_ref[0, rows, hl], preferred_element_type=F32)
                 + jnp.dot(qq, r_scr[hh * nc + c].astype(BF16), preferred_element_type=F32))
            o = o * lax.rsqrt(jnp.mean(o * o, axis=-1, keepdims=True) + EPS)
            o_ref[0, rows, hl] = (o * g_ref[0, rows, hl].astype(F32)).astype(BF16)


def _ret_call(log_gamma, bq, bk, bv, gb):
    batch, seq, _ = bq.shape
    nc = seq // RET_CHUNK
    spec = pl.BlockSpec((1, seq, RET_HEADS * LANES), lambda b, h: (b, 0, h))
    state_shape = (RET_HEADS * nc, LANES, V_DIM)
    return pl.pallas_call(
        functools.partial(_ret_kernel, seq=seq),
        grid=(batch, HEADS // RET_HEADS),
        in_specs=[pl.BlockSpec(memory_space=pltpu.SMEM), spec, spec, spec, spec],
        out_specs=spec,
        out_shape=jax.ShapeDtypeStruct((batch, seq, D_BRANCH), BF16),
        scratch_shapes=[pltpu.VMEM(state_shape, F32), pltpu.VMEM(state_shape, F32)],
        compiler_params=pltpu.CompilerParams(
            dimension_semantics=("arbitrary", "arbitrary"), vmem_limit_bytes=VMEM_LIMIT),
        name="retention",
    )(log_gamma, bq, bk, bv, gb)


def _out_kernel(oa_ref, ob_ref, cu_ref, cup_ref, cun_ref, gc_ref, x_ref, wo_ref, pw_ref, ps_ref,
                fnw_ref, o_ref, ext_scr, pooled_scr, *, seq, final):
    tm = cu_ref.shape[0]
    tiles_per_seq = seq // tm
    it = pl.program_id(0) % tiles_per_seq

    ext_scr[0:POOL_HALO, :] = jnp.where(it == 0, 0.0, cup_ref[...])
    ext_scr[POOL_HALO:POOL_HALO + tm, :] = cu_ref[...]
    ext_scr[POOL_HALO + tm:, :] = jnp.where(it == tiles_per_seq - 1, 0.0, cun_ref[...])

    n = tm + 2 * POOL_HALO
    body = slice(POOL_HALO, POOL_HALO + tm)
    for g, w in enumerate(POOL_WINDOWS):
        cols = slice(g * LANES, (g + 1) * LANES)
        run = {1: ext_scr[:, cols]}
        k = 1
        while 2 * k <= min(w, POOL_HALO):
            run[2 * k] = run[k] + pltpu.roll(run[k], n - k, 0)
            k *= 2
        if w == 2 * POOL_HALO:
            wsum = run[POOL_HALO][0:tm] + run[POOL_HALO][body]
        else:
            wsum = pltpu.roll(run[w], w // 2, 0)[body]
        u = cu_ref[:, cols]
        pooled_scr[:, cols] = (wsum * (1.0 / w) - u).astype(BF16)
        for r0 in (0, tm - EDGE_ROWS):
            edge = slice(r0, r0 + EDGE_ROWS)
            pos = it * tm + r0 + lax.broadcasted_iota(jnp.int32, (EDGE_ROWS, 1), 0)
            lo = jnp.maximum(pos - w // 2, 0)
            hi = jnp.minimum(pos + (w - w // 2), seq)
            count = (hi - lo).astype(F32)
            pooled_scr[edge, cols] = (wsum[edge] / count - u[edge]).astype(BF16)

    out = x_ref[...] + jnp.dot(oa_ref[...], wo_ref[0:D_BRANCH, :], preferred_element_type=F32)
    out = out + jnp.dot(ob_ref[...], wo_ref[D_BRANCH:2 * D_BRANCH, :],
                        preferred_element_type=F32)
    ys = [jnp.dot(pooled_scr[:, g * LANES:(g + 1) * LANES], pw_ref[g],
                  preferred_element_type=F32) for g in range(len(POOL_WINDOWS))]
    yc = jnp.concatenate(ys, axis=1) * ps_ref[...] * gc_ref[...].astype(F32)
    out = out + jnp.dot(yc.astype(BF16), wo_ref[2 * D_BRANCH:, :], preferred_element_type=F32)
    if final:
        out = out * lax.rsqrt(jnp.mean(out * out, axis=-1, keepdims=True) + EPS) * fnw_ref[...]
    o_ref[...] = out


def _out_call(oa, ob, cu, gc, x2, w_out, pool_w, pool_scale, final_norm_w, seq, final):
    rows = x2.shape[0]
    tm = OUT_ROWS
    halo_blocks = tm // POOL_HALO
    last_halo = rows // POOL_HALO - 1
    row_spec = lambda n: pl.BlockSpec((tm, n), lambda i: (i, 0))
    const = lambda shape: pl.BlockSpec(shape, lambda i: (0,) * len(shape))
    return pl.pallas_call(
        functools.partial(_out_kernel, seq=seq, final=final),
        grid=(rows // tm,),
        in_specs=[
            row_spec(D_BRANCH), row_spec(D_BRANCH), row_spec(D_BRANCH),
            pl.BlockSpec((POOL_HALO, D_BRANCH),
                         lambda i: (jnp.maximum(i * halo_blocks - 1, 0), 0)),
            pl.BlockSpec((POOL_HALO, D_BRANCH),
                         lambda i: (jnp.minimum((i + 1) * halo_blocks, last_halo), 0)),
            row_spec(D_BRANCH), row_spec(D_MODEL),
            const((3 * D_BRANCH, D_MODEL)),
            const((len(POOL_WINDOWS), LANES, LANES)),
            const((1, D_BRANCH)),
            const((1, D_MODEL)),
        ],
        out_specs=row_spec(D_MODEL),
        out_shape=jax.ShapeDtypeStruct((rows, D_MODEL), F32),
        scratch_shapes=[pltpu.VMEM((tm + 2 * POOL_HALO, D_BRANCH), F32),
                        pltpu.VMEM((tm, D_BRANCH), BF16)],
        compiler_params=pltpu.CompilerParams(
            dimension_semantics=("arbitrary",), vmem_limit_bytes=VMEM_LIMIT),
        name="pool_out",
    )(oa, ob, cu, cu, cu, gc, x2, w_out, pool_w, pool_scale, final_norm_w)


def _t5_bucket(rel):
    half = NUM_BUCKETS // 2
    max_exact = half // 2
    ret = jnp.where(rel > 0, half, 0)
    n = jnp.abs(rel)
    nf = jnp.maximum(n, 1).astype(F32)
    large = max_exact + (jnp.log(nf / max_exact) / math.log(MAX_DISTANCE / max_exact)
                         * (half - max_exact)).astype(jnp.int32)
    large = jnp.minimum(large, half - 1)
    return ret + jnp.where(n < max_exact, n, large)


def _bias_kernel(u_ref, o_ref):
    t = BIAS_T
    for r in range(u_ref.shape[1]):
        x = jnp.broadcast_to(u_ref[0, r:r + 1, :], (t, 2 * t))
        o_ref[0, r] = pltpu.roll(x, t + 1, 1, stride=1, stride_axis=0)[:, :t]


def _bias_tiles(rel_bias, seq):
    t = BIAS_T
    assert t >= MAX_DISTANCE and ATTN_Q % t == 0
    off = jnp.arange(2 * t, dtype=jnp.int32) - (t - 1)
    rel = jnp.stack([off - t, off, off + t, jnp.full_like(off, -seq), jnp.full_like(off, seq)])
    u = rel_bias.astype(F32)[_t5_bucket(rel)].transpose(2, 0, 1) * LOG2E
    u = jnp.roll(u[..., ::-1], -1, axis=-1)
    n_tiles = rel.shape[0]
    return pl.pallas_call(
        _bias_kernel,
        grid=(HEADS,),
        in_specs=[pl.BlockSpec((1, n_tiles, 2 * t), lambda h: (h, 0, 0))],
        out_specs=pl.BlockSpec((1, n_tiles, t, t), lambda h: (h, 0, 0, 0)),
        out_shape=jax.ShapeDtypeStruct((HEADS, n_tiles, t, t), F32),
        compiler_params=pltpu.CompilerParams(dimension_semantics=("arbitrary",)),
        name="bias_tiles",
    )(u)


def _rotary_tables(seq):
    half = QK_DIM // 2
    theta = 1.0 / (ROPE_BASE ** jnp.linspace(0.0, 1.0, half, dtype=F32))
    ang = jnp.arange(seq, dtype=F32)[:, None] * theta[None, :]
    cos = jnp.cos(ang)
    sin = jnp.sin(ang)
    reps = LANES // QK_DIM
    cos_t = jnp.tile(jnp.concatenate([cos, cos], axis=1), (1, reps))
    sin_t = jnp.tile(jnp.concatenate([-sin, sin], axis=1), (1, reps))
    return cos_t, sin_t


def kernel(x, norm_w, w_in, diff_lambda, diff_subln_w, ret_decay_logit, pool_w, pool_scale, w_out,
           rel_bias, final_norm_w):
    batch, seq, d_model = x.shape
    assert d_model == D_MODEL and seq % PROJ_ROWS == 0 and seq % OUT_ROWS == 0
    assert seq % (ATTN_Q * ATTN_SUB) == 0
    rows = batch * seq
    cos_t, sin_t = _rotary_tables(seq)
    bias_t = _bias_tiles(rel_bias, seq)
    fnw = final_norm_w.reshape(1, D_MODEL).astype(F32)

    h = x.reshape(rows, D_MODEL)
    for l in range(DEPTH):
        lam_init = 0.8 - 0.6 * math.exp(-0.3 * l)
        lf = diff_lambda[l].astype(F32)
        lam = (jnp.exp(jnp.sum(lf[0] * lf[1])) - jnp.exp(jnp.sum(lf[2] * lf[3])) + lam_init)
        log_gamma = jax.nn.log_sigmoid(ret_decay_logit[l].astype(F32))

        aq, ak, av, ga, bq, bk, bv, gb, cu, gc = _proj_call(
            h, norm_w[l].reshape(1, D_MODEL).astype(F32), w_in[l].astype(BF16), cos_t, sin_t, seq)
        to3 = lambda a: a.reshape(batch, seq, a.shape[-1])
        oa = _attn_call(lam.reshape(1), to3(aq), to3(ak), av, to3(ga), bias_t,
                        diff_subln_w[l].reshape(1, V_DIM).astype(F32), lam_init)
        ob = _ret_call(log_gamma, to3(bq), to3(bk), to3(bv), to3(gb))
        h = _out_call(oa.reshape(rows, D_BRANCH), ob.reshape(rows, D_BRANCH), cu, gc, h,
                      w_out[l].astype(BF16), pool_w[l].astype(BF16),
                      pool_scale[l].reshape(1, D_BRANCH).astype(F32), fnw, seq,
                      final=(l == DEPTH - 1))
    return h.reshape(batch, seq, D_MODEL)
```
